```python
import jax, jax.numpy as jnp
from jax import lax
import numpy as np

D_MODEL = 2048
BATCH = 2
SEQ = 4096
DEPTH = 4

RET_HEADS = 8
RET_QK_DIM = 128
RET_V_DIM = 256
RET_QK_WIDTH = RET_HEADS * RET_QK_DIM
RET_V_WIDTH = RET_HEADS * RET_V_DIM
RET_CHUNK = 128
ROPE_BASE = 10000.0
POOL_WINDOWS = (2, 4, 8, 16)
POOL_GROUPS = len(POOL_WINDOWS)
POOL_GROUP_DIM = D_MODEL // 8
POOL_WIDTH = POOL_GROUPS * POOL_GROUP_DIM
N_BRANCHES = 2
IN_SPLITS = (
    RET_QK_WIDTH,
    2 * RET_QK_WIDTH,
    2 * RET_QK_WIDTH + RET_V_WIDTH,
    2 * RET_QK_WIDTH + 2 * RET_V_WIDTH,
    2 * RET_QK_WIDTH + 2 * RET_V_WIDTH + POOL_WIDTH,
)
IN_WIDTH = 2 * RET_QK_WIDTH + 2 * RET_V_WIDTH + POOL_WIDTH + N_BRANCHES * D_MODEL
D_FF = ((8 * D_MODEL + 3 * 256 - 1) // (3 * 256)) * 256
NORM_EPS = 1e-6

kernel_name = "hybrid_retention_pool_gated_block"


def rmsnorm(x, g):
    xf = x.astype(jnp.float32)
    xf = xf * lax.rsqrt(jnp.mean(xf * xf, axis=-1, keepdims=True) + NORM_EPS)
    return (xf * g.astype(jnp.float32)).astype(x.dtype)


def rotary(t, positions):
    half = t.shape[-1] // 2
    inv_freq = ROPE_BASE ** (-jnp.arange(half, dtype=jnp.float32) / half)
    ang = positions.astype(jnp.float32)[..., None] * inv_freq
    cos = jnp.cos(ang)[:, :, None, :].astype(t.dtype)
    sin = jnp.sin(ang)[:, :, None, :].astype(t.dtype)
    t1, t2 = t[..., :half], t[..., half:]
    return jnp.concatenate([t1 * cos - t2 * sin, t2 * cos + t1 * sin], axis=-1)


def retention_chunkwise(q, k, v):
    b, s, h, dk = q.shape
    dv = v.shape[-1]
    n = s // RET_CHUNK
    c = RET_CHUNK
    dt = q.dtype
    qc = q.reshape(b, n, c, h, dk).transpose(0, 3, 1, 2, 4)
    kc = k.reshape(b, n, c, h, dk).transpose(0, 3, 1, 2, 4)
    vc = v.reshape(b, n, c, h, dv).transpose(0, 3, 1, 2, 4)
    log_gamma = jnp.log(1.0 - 2.0 ** (-5.0 - jnp.arange(h, dtype=jnp.float32)))
    idx = jnp.arange(c, dtype=jnp.float32)
    diff = idx[:, None] - idx[None, :]
    decay_mask = jnp.where(diff[None] >= 0,
                           jnp.exp(jnp.maximum(diff, 0.0)[None] * log_gamma[:, None, None]),
                           0.0).astype(dt)
    xi = jnp.exp((idx + 1.0)[None, :] * log_gamma[:, None]).astype(dt)
    zeta = jnp.exp((c - 1.0 - idx)[None, :] * log_gamma[:, None]).astype(dt)
    chunk_decay = jnp.exp(c * log_gamma).astype(dt)
    scores = jnp.einsum('bhncd,bhnmd->bhncm', qc, kc) * decay_mask[None, :, None]
    intra = jnp.einsum('bhncm,bhnme->bhnce', scores, vc)
    kv = jnp.einsum('bhncd,bhnce->bhnde', kc * zeta[None, :, None, :, None], vc)

    def step(state, kv_i):
        new_state = state * chunk_decay[None, :, None, None] + kv_i
        return new_state, state

    init = jnp.zeros((b, h, dk, dv), dtype=kv.dtype)
    _, prev_states = lax.scan(step, init, jnp.moveaxis(kv, 2, 0))
    prev_states = jnp.moveaxis(prev_states, 0, 2)
    cross = jnp.einsum('bhncd,bhnde->bhnce', qc * xi[None, :, None, :, None], prev_states)
    y = (intra + cross).reshape(b, h, s, dv)
    yf = y.astype(jnp.float32)
    yf = yf * lax.rsqrt(jnp.mean(yf * yf, axis=-1, keepdims=True) + NORM_EPS)
    return yf.astype(dt).transpose(0, 2, 1, 3)


def causal_multiscale_pool(u, pool_mix, pool_scale):
    b, s, _ = u.shape
    ug = u.reshape(b, s, POOL_GROUPS, POOL_GROUP_DIM)
    cs = jnp.cumsum(ug.astype(jnp.float32), axis=1)
    cs0 = jnp.concatenate([jnp.zeros((b, 1, POOL_GROUPS, POOL_GROUP_DIM), jnp.float32), cs], axis=1)
    t = jnp.arange(s, dtype=jnp.float32)
    pooled = []
    for gi, w in enumerate(POOL_WINDOWS):
        upper = cs0[:, 1:, gi]
        lower = jnp.concatenate([jnp.zeros((b, w - 1, POOL_GROUP_DIM), jnp.float32),
                                 cs0[:, :s + 1 - w, gi]], axis=1)
        count = jnp.minimum(t + 1.0, float(w))[None, :, None]
        pooled.append((upper - lower) / count - ug[:, :, gi].astype(jnp.float32))
    pooled = jnp.stack(pooled, axis=2).astype(u.dtype)
    mixed = jnp.einsum('bsgc,gcd->bsgd', pooled, pool_mix)
    return mixed.reshape(b, s, POOL_WIDTH) * pool_scale


def setup_inputs(seed: int = 0) -> dict:
    key = jax.random.key(seed)
    ks = jax.random.split(key, 14)
    f32 = jnp.float32

    def nrm(k, shape, fan_in):
        return jax.random.normal(k, shape, f32) * (fan_in ** -0.5)

    def gain(k, shape):
        return 1.0 + 0.02 * jax.random.normal(k, shape, f32)

    x = jax.random.normal(ks[0], (BATCH, SEQ, D_MODEL), f32)
    positions = jnp.broadcast_to(jnp.arange(SEQ, dtype=jnp.int32)[None, :], (BATCH, SEQ))
    return {
        "x": x,
        "positions": positions,
        "norm1": gain(ks[1], (DEPTH, D_MODEL)),
        "w_in": nrm(ks[2], (DEPTH, D_MODEL, IN_WIDTH), D_MODEL),
        "pool_mix": nrm(ks[3], (DEPTH, POOL_GROUPS, POOL_GROUP_DIM, POOL_GROUP_DIM), POOL_GROUP_DIM),
        "pool_scale": gain(ks[4], (DEPTH, POOL_WIDTH)),
        "w_ret_up": nrm(ks[5], (DEPTH, RET_V_WIDTH, D_MODEL), RET_V_WIDTH),
        "w_pool_up": nrm(ks[6], (DEPTH, POOL_WIDTH, D_MODEL), POOL_WIDTH),
        "w_o": nrm(ks[7], (DEPTH, D_MODEL, D_MODEL), D_MODEL),
        "norm2": gain(ks[8], (DEPTH, D_MODEL)),
        "w_gu": nrm(ks[9], (DEPTH, D_MODEL, 2 * D_FF), D_MODEL),
        "w_down": nrm(ks[10], (DEPTH, D_FF, D_MODEL), D_FF),
        "norm_f": gain(ks[11], (D_MODEL,)),
    }


def reference(x, positions, norm1, w_in, pool_mix, pool_scale, w_ret_up, w_pool_up, w_o,
              norm2, w_gu, w_down, norm_f):
    b, s, _ = x.shape
    q_scale = RET_QK_DIM ** -0.5
    for l in range(DEPTH):
        xn = rmsnorm(x, norm1[l])
        proj = xn @ w_in[l]
        q, k, v, g_ret, u_pool, gate_logits = jnp.split(proj, IN_SPLITS, axis=-1)
        q = rotary(q.reshape(b, s, RET_HEADS, RET_QK_DIM), positions) * q_scale
        k = rotary(k.reshape(b, s, RET_HEADS, RET_QK_DIM), positions)
        v = v.reshape(b, s, RET_HEADS, RET_V_DIM)
        y_ret = retention_chunkwise(q, k, v).reshape(b, s, RET_V_WIDTH) * jax.nn.silu(g_ret)
        y_pool = causal_multiscale_pool(u_pool, pool_mix[l], pool_scale[l])
        gates = jax.nn.sigmoid(gate_logits.reshape(b, s, N_BRANCHES, D_MODEL))
        merged = gates[:, :, 0] * (y_ret @ w_ret_up[l]) + gates[:, :, 1] * (y_pool @ w_pool_up[l])
        x = x + merged @ w_o[l]
        xn2 = rmsnorm(x, norm2[l])
        gate_h, up_h = jnp.split(xn2 @ w_gu[l], 2, axis=-1)
        x = x + (jax.nn.silu(gate_h) * up_h) @ w_down[l]
    return rmsnorm(x, norm_f)
```

```python
import functools

import jax
import jax.numpy as jnp
from jax import lax
from jax.experimental import pallas as pl
from jax.experimental.pallas import tpu as pltpu

D_MODEL = 2048
RET_HEADS = 8
RET_QK_DIM = 128
RET_V_DIM = 256
RET_QK_WIDTH = RET_HEADS * RET_QK_DIM
RET_V_WIDTH = RET_HEADS * RET_V_DIM
RET_CHUNK = 128
ROPE_BASE = 10000.0
POOL_WINDOWS = (2, 4, 8, 16)
POOL_GROUPS = len(POOL_WINDOWS)
POOL_GROUP_DIM = D_MODEL // 8
POOL_WIDTH = POOL_GROUPS * POOL_GROUP_DIM
N_BRANCHES = 2
IN_WIDTH = 2 * RET_QK_WIDTH + 2 * RET_V_WIDTH + POOL_WIDTH + N_BRANCHES * D_MODEL
NORM_EPS = 1e-6

Q_OFF = 0
K_OFF = RET_QK_WIDTH
V_OFF = 2 * RET_QK_WIDTH
G_OFF = V_OFF + RET_V_WIDTH
U_OFF = G_OFF + RET_V_WIDTH
GATE_OFF = U_OFF + POOL_WIDTH

POOL_HALO = 16
assert POOL_HALO >= max(POOL_WINDOWS) and POOL_HALO % 8 == 0

VMEM_LIMIT_BYTES = 56 * 1024 * 1024

BF16 = jnp.bfloat16
F32 = jnp.float32


def _params(n_axes):
    return pltpu.CompilerParams(
        dimension_semantics=("arbitrary",) * n_axes,
        vmem_limit_bytes=VMEM_LIMIT_BYTES,
    )


def _rope_kernel(pos_ref, freq_ref, cos_ref, sin_ref):
    ang = pos_ref[...] * freq_ref[...]
    lane = lax.broadcasted_iota(jnp.int32, ang.shape, 1)
    sin = jnp.sin(ang)
    cos_ref[...] = jnp.cos(ang)
    sin_ref[...] = jnp.where(lane < RET_QK_DIM // 2, -sin, sin)


def _rope_tables(positions):
    t = positions.size
    half = RET_QK_DIM // 2
    inv_freq = ROPE_BASE ** (-jnp.arange(half, dtype=F32) / half)
    freq = jnp.concatenate([inv_freq, inv_freq])[None, :]
    pos = jnp.broadcast_to(positions.reshape(t, 1).astype(F32), (t, RET_QK_DIM))
    tm = 1024
    spec = pl.BlockSpec((tm, RET_QK_DIM), lambda i: (i, 0))
    return pl.pallas_call(
        _rope_kernel,
        out_shape=(jax.ShapeDtypeStruct((t, RET_QK_DIM), F32),) * 2,
        grid=(t // tm,),
        in_specs=[spec, pl.BlockSpec((1, RET_QK_DIM), lambda i: (0, 0))],
        out_specs=(spec, spec),
        compiler_params=_params(1),
        name="rope_tables",
    )(pos, freq)


def _rmsnorm_to(x_ref, g_ref, out_ref):
    x = x_ref[...]
    ms = jnp.mean(x * x, axis=-1, keepdims=True)
    out_ref[...] = (x * lax.rsqrt(ms + NORM_EPS) * g_ref[...]).astype(out_ref.dtype)


def _rotary(t, cos, sin):
    outs = []
    for h in range(t.shape[1] // RET_QK_DIM):
        th = t[:, h * RET_QK_DIM:(h + 1) * RET_QK_DIM]
        outs.append(th * cos + pltpu.roll(th, RET_QK_DIM // 2, axis=1) * sin)
    return jnp.concatenate(outs, axis=1)


def _inproj_kernel(x_ref, g_ref, w_ref, cos_ref, sin_ref, pmix_ref, pscale_ref,
                   o_ref, xn_ref, halo_ref, *, tn, tiles_per_seq):
    i = pl.program_id(0)
    j = pl.program_id(1)
    tm = x_ref.shape[0]

    @pl.when(j == 0)
    def _():
        _rmsnorm_to(x_ref, g_ref, xn_ref)

    acc = jnp.dot(xn_ref[...], w_ref[...], preferred_element_type=F32)

    @pl.when(j == Q_OFF // tn)
    def _():
        q = _rotary(acc, cos_ref[...], sin_ref[...]) * (RET_QK_DIM ** -0.5)
        o_ref[...] = q.astype(o_ref.dtype)

    @pl.when(j == K_OFF // tn)
    def _():
        o_ref[...] = _rotary(acc, cos_ref[...], sin_ref[...]).astype(o_ref.dtype)

    @pl.when((j >= V_OFF // tn) & (j < G_OFF // tn))
    def _():
        o_ref[...] = acc.astype(o_ref.dtype)

    @pl.when((j >= G_OFF // tn) & (j < U_OFF // tn))
    def _():
        o_ref[...] = (acc * jax.nn.sigmoid(acc)).astype(o_ref.dtype)

    @pl.when(j == U_OFF // tn)
    def _():
        seq_tile = i % tiles_per_seq

        @pl.when(seq_tile == 0)
        def _():
            halo_ref[...] = jnp.zeros_like(halo_ref)

        halo = halo_ref[...]
        halo_ref[...] = acc[tm - POOL_HALO:, :]
        row = lax.broadcasted_iota(jnp.int32, (tm, 1), 0) + seq_tile * tm
        pos1 = (row + 1).astype(F32)
        for gi, w in enumerate(POOL_WINDOWS):
            cols = slice(gi * POOL_GROUP_DIM, (gi + 1) * POOL_GROUP_DIM)
            u = acc[:, cols]
            s = jnp.concatenate([halo[:, cols], u], axis=0)
            shift = 1
            while shift < w:
                s = s + pltpu.roll(s, shift, axis=0)
                shift *= 2
            count = jnp.minimum(pos1, float(w))
            pooled = s[POOL_HALO:, :] / count - u
            mixed = jnp.dot(pooled.astype(BF16), pmix_ref[gi], preferred_element_type=F32)
            o_ref[:, cols] = (mixed * pscale_ref[:, cols]).astype(o_ref.dtype)

    @pl.when(j >= GATE_OFF // tn)
    def _():
        o_ref[...] = jax.nn.sigmoid(acc).astype(o_ref.dtype)


def _inproj(x, norm1, w_in, cos, sin, pool_mix, pool_scale, layer, seq):
    t = x.shape[0]
    tm, tn = 1024, 1024
    assert seq % tm == 0 and POOL_WIDTH == tn and RET_QK_WIDTH == tn
    kern = functools.partial(_inproj_kernel, tn=tn, tiles_per_seq=seq // tm)
    return pl.pallas_call(
        kern,
        out_shape=jax.ShapeDtypeStruct((t, IN_WIDTH), BF16),
        grid=(t // tm, IN_WIDTH // tn),
        in_specs=[
            pl.BlockSpec((tm, D_MODEL), lambda i, j: (i, 0)),
            pl.BlockSpec((None, 1, D_MODEL), lambda i, j: (layer, 0, 0)),
            pl.BlockSpec((None, D_MODEL, tn), lambda i, j: (layer, 0, j)),
            pl.BlockSpec((tm, RET_QK_DIM), lambda i, j: (i, 0)),
            pl.BlockSpec((tm, RET_QK_DIM), lambda i, j: (i, 0)),
            pl.BlockSpec((None, POOL_GROUPS, POOL_GROUP_DIM, POOL_GROUP_DIM),
                         lambda i, j: (layer, 0, 0, 0)),
            pl.BlockSpec((None, 1, POOL_WIDTH), lambda i, j: (layer, 0, 0)),
        ],
        out_specs=pl.BlockSpec((tm, tn), lambda i, j: (i, j)),
        scratch_shapes=[
            pltpu.VMEM((tm, D_MODEL), BF16),
            pltpu.VMEM((POOL_HALO, POOL_WIDTH), F32),
        ],
        compiler_params=_params(2),
        name="in_proj",
    )(x, norm1, w_in, cos, sin, pool_mix, pool_scale)


def _retention_tables():
    c = RET_CHUNK
    log_gamma = jnp.log(1.0 - 2.0 ** (-5.0 - jnp.arange(RET_HEADS, dtype=F32)))
    idx = jnp.arange(c, dtype=F32)
    diff = idx[:, None] - idx[None, :]
    decay_mask = jnp.where(diff[None] >= 0,
                           jnp.exp(jnp.maximum(diff, 0.0)[None] * log_gamma[:, None, None]),
                           0.0)
    xi = jnp.exp((idx + 1.0)[None, :] * log_gamma[:, None])
    zeta = jnp.exp((c - 1.0 - idx)[None, :] * log_gamma[:, None])
    chunk_decay = jnp.exp(c * log_gamma)
    xi = jnp.broadcast_to(xi[:, :, None], (RET_HEADS, c, RET_QK_DIM))
    zeta = jnp.broadcast_to(zeta[:, :, None], (RET_HEADS, c, RET_QK_DIM))
    chunk_decay = jnp.broadcast_to(chunk_decay[:, None, None], (RET_HEADS, 1, RET_V_DIM))
    return decay_mask, xi, zeta, chunk_decay


def _retention_kernel(q_ref, k_ref, v_ref, g_ref, dmask_ref, xi_ref, zeta_ref, cdecay_ref,
                      o_ref, state_ref):
    c = RET_CHUNK

    @pl.when(pl.program_id(2) == 0)
    def _():
        state_ref[...] = jnp.zeros_like(state_ref)

    dmask = dmask_ref[...]
    xi = xi_ref[...]
    zeta = zeta_ref[...]
    cdecay = cdecay_ref[...]
    for n in range(q_ref.shape[0] // c):
        rows = slice(n * c, (n + 1) * c)
        q = q_ref[rows, :]
        k = k_ref[rows, :]
        v = v_ref[rows, :]
        state = state_ref[...]
        scores = lax.dot_general(q, k, (((1,), (1,)), ((), ())), preferred_element_type=F32)
        scores = scores * dmask
        intra = jnp.dot(scores.astype(BF16), v, preferred_element_type=F32)
        qx = (q.astype(F32) * xi).astype(BF16)
        cross = jnp.dot(qx, state.astype(BF16), preferred_element_type=F32)
        kz = (k.astype(F32) * zeta).astype(BF16)
        kv = lax.dot_general(kz, v, (((0,), (0,)), ((), ())), preferred_element_type=F32)
        state_ref[...] = state * cdecay + kv
        y = intra + cross
        y = y * lax.rsqrt(jnp.mean(y * y, axis=-1, keepdims=True) + NORM_EPS)
        o_ref[rows, :] = (y * g_ref[rows, :].astype(F32)).astype(o_ref.dtype)


def _retention(proj, tables, batch, seq):
    t = proj.shape[0]
    ts = 1024
    spb = seq // ts
    dmask, xi, zeta, cdecay = tables
    qk_blk = lambda off: pl.BlockSpec(
        (ts, RET_QK_DIM), lambda b, h, s: (b * spb + s, off // RET_QK_DIM + h))
    v_blk = lambda off: pl.BlockSpec(
        (ts, RET_V_DIM), lambda b, h, s: (b * spb + s, off // RET_V_DIM + h))
    head_blk = lambda shape: pl.BlockSpec((None,) + shape, lambda b, h, s: (h, 0, 0))
    return pl.pallas_call(
        _retention_kernel,
        out_shape=jax.ShapeDtypeStruct((t, RET_V_WIDTH), BF16),
        grid=(batch, RET_HEADS, spb),
        in_specs=[
            qk_blk(Q_OFF), qk_blk(K_OFF), v_blk(V_OFF), v_blk(G_OFF),
            head_blk((RET_CHUNK, RET_CHUNK)),
            head_blk((RET_CHUNK, RET_QK_DIM)),
            head_blk((RET_CHUNK, RET_QK_DIM)),
            head_blk((1, RET_V_DIM)),
        ],
        out_specs=v_blk(0),
        scratch_shapes=[pltpu.VMEM((RET_QK_DIM, RET_V_DIM), F32)],
        compiler_params=_params(3),
        name="retention",
    )(proj, proj, proj, proj, dmask, xi, zeta, cdecay)


def _merge_kernel(yr_ref, yp_ref, g0_ref, g1_ref, wr_ref, wp_ref, o_ref):
    r = jnp.dot(yr_ref[...], wr_ref[...], preferred_element_type=F32)
    p = jnp.dot(yp_ref[...], wp_ref[...], preferred_element_type=F32)
    merged = g0_ref[...].astype(F32) * r + g1_ref[...].astype(F32) * p
    o_ref[...] = merged.astype(o_ref.dtype)


def _merge(y_ret, proj, w_ret_up, w_pool_up, layer):
    t = y_ret.shape[0]
    tm, tn = 1024, 1024
    assert POOL_WIDTH == tn
    return pl.pallas_call(
        _merge_kernel,
        out_shape=jax.ShapeDtypeStruct((t, D_MODEL), BF16),
        grid=(t // tm, D_MODEL // tn),
        in_specs=[
            pl.BlockSpec((tm, RET_V_WIDTH), lambda i, j: (i, 0)),
            pl.BlockSpec((tm, POOL_WIDTH), lambda i, j: (i, U_OFF // POOL_WIDTH)),
            pl.BlockSpec((tm, tn), lambda i, j: (i, GATE_OFF // tn + j)),
            pl.BlockSpec((tm, tn), lambda i, j: (i, (GATE_OFF + D_MODEL) // tn + j)),
            pl.BlockSpec((None, RET_V_WIDTH, tn), lambda i, j: (layer, 0, j)),
            pl.BlockSpec((None, POOL_WIDTH, tn), lambda i, j: (layer, 0, j)),
        ],
        out_specs=pl.BlockSpec((tm, tn), lambda i, j: (i, j)),
        compiler_params=_params(2),
        name="merge",
    )(y_ret, proj, proj, proj, w_ret_up, w_pool_up)


def _residual_matmul_kernel(x_ref, a_ref, w_ref, o_ref):
    o_ref[...] = x_ref[...] + jnp.dot(a_ref[...], w_ref[...], preferred_element_type=F32)


def _residual_matmul(x, a, w, layer, tm, tn, name):
    t, k = a.shape
    return pl.pallas_call(
        _residual_matmul_kernel,
        out_shape=jax.ShapeDtypeStruct(x.shape, x.dtype),
        grid=(t // tm, D_MODEL // tn),
        in_specs=[
            pl.BlockSpec((tm, tn), lambda i, j: (i, j)),
            pl.BlockSpec((tm, k), lambda i, j: (i, 0)),
            pl.BlockSpec((None, k, tn), lambda i, j: (layer, 0, j)),
        ],
        out_specs=pl.BlockSpec((tm, tn), lambda i, j: (i, j)),
        compiler_params=_params(2),
        name=name,
    )(x, a, w)


def _ffn_up_kernel(x_ref, g_ref, wg_ref, wu_ref, o_ref, xn_ref):
    @pl.when(pl.program_id(1) == 0)
    def _():
        _rmsnorm_to(x_ref, g_ref, xn_ref)

    xn = xn_ref[...]
    gate = jnp.dot(xn, wg_ref[...], preferred_element_type=F32)
    up = jnp.dot(xn, wu_ref[...], preferred_element_type=F32)
    o_ref[...] = (gate * jax.nn.sigmoid(gate) * up).astype(o_ref.dtype)


def _ffn_up(x, norm2, w_gu, layer):
    t = x.shape[0]
    d_ff = w_gu.shape[2] // 2
    tm, tn = 1024, 512
    assert d_ff % tn == 0
    return pl.pallas_call(
        _ffn_up_kernel,
        out_shape=jax.ShapeDtypeStruct((t, d_ff), BF16),
        grid=(t // tm, d_ff // tn),
        in_specs=[
            pl.BlockSpec((tm, D_MODEL), lambda i, j: (i, 0)),
            pl.BlockSpec((None, 1, D_MODEL), lambda i, j: (layer, 0, 0)),
            pl.BlockSpec((None, D_MODEL, tn), lambda i, j: (layer, 0, j)),
            pl.BlockSpec((None, D_MODEL, tn), lambda i, j: (layer, 0, d_ff // tn + j)),
        ],
        out_specs=pl.BlockSpec((tm, tn), lambda i, j: (i, j)),
        scratch_shapes=[pltpu.VMEM((tm, D_MODEL), BF16)],
        compiler_params=_params(2),
        name="ffn_up",
    )(x, norm2, w_gu, w_gu)


def _final_norm_kernel(x_ref, g_ref, o_ref):
    _rmsnorm_to(x_ref, g_ref, o_ref)


def _final_norm(x, norm_f):
    t = x.shape[0]
    tm = 512
    return pl.pallas_call(
        _final_norm_kernel,
        out_shape=jax.ShapeDtypeStruct(x.shape, x.dtype),
        grid=(t // tm,),
        in_specs=[
            pl.BlockSpec((tm, D_MODEL), lambda i: (i, 0)),
            pl.BlockSpec((1, D_MODEL), lambda i: (0, 0)),
        ],
        out_specs=pl.BlockSpec((tm, D_MODEL), lambda i: (i, 0)),
        compiler_params=_params(1),
        name="final_norm",
    )(x, norm_f)


def kernel(x, positions, norm1, w_in, pool_mix, pool_scale, w_ret_up, w_pool_up, w_o,
           norm2, w_gu, w_down, norm_f):
    batch, seq, _ = x.shape
    depth = w_in.shape[0]
    t = batch * seq
    xt = x.reshape(t, D_MODEL)

    w_in_b = w_in.astype(BF16)
    pool_mix_b = pool_mix.astype(BF16)
    w_ret_up_b = w_ret_up.astype(BF16)
    w_pool_up_b = w_pool_up.astype(BF16)
    w_o_b = w_o.astype(BF16)
    w_gu_b = w_gu.astype(BF16)
    w_down_b = w_down.astype(BF16)
    norm1_r = norm1.reshape(depth, 1, D_MODEL)
    norm2_r = norm2.reshape(depth, 1, D_MODEL)
    pool_scale_r = pool_scale.reshape(depth, 1, POOL_WIDTH)

    cos, sin = _rope_tables(positions)
    tables = _retention_tables()

    for layer in range(depth):
        proj = _inproj(xt, norm1_r, w_in_b, cos, sin, pool_mix_b, pool_scale_r, layer, seq)
        y_ret = _retention(proj, tables, batch, seq)
        merged = _merge(y_ret, proj, w_ret_up_b, w_pool_up_b, layer)
        xt = _residual_matmul(xt, merged, w_o_b, layer, 1024, 1024, "out_proj")
        hidden = _ffn_up(xt, norm2_r, w_gu_b, layer)
        xt = _residual_matmul(xt, hidden, w_down_b, layer, 1024, 512, "ffn_down")
    out = _final_norm(xt, norm_f.reshape(1, D_MODEL))
    return out.reshape(batch, seq, D_MODEL)
```

```python
import functools

import jax
import jax.numpy as jnp
from jax import lax
from jax.experimental import pallas as pl
from jax.experimental.pallas import tpu as pltpu

D_MODEL = 2048
RET_HEADS = 8
RET_QK_DIM = 128
RET_V_DIM = 256
RET_QK_WIDTH = RET_HEADS * RET_QK_DIM
RET_V_WIDTH = RET_HEADS * RET_V_DIM
RET_CHUNK = 128
ROPE_BASE = 10000.0
POOL_WINDOWS = (2, 4, 8, 16)
POOL_GROUPS = len(POOL_WINDOWS)
POOL_GROUP_DIM = D_MODEL // 8
POOL_WIDTH = POOL_GROUPS * POOL_GROUP_DIM
N_BRANCHES = 2
IN_WIDTH = 2 * RET_QK_WIDTH + 2 * RET_V_WIDTH + POOL_WIDTH + N_BRANCHES * D_MODEL
NORM_EPS = 1e-6

Q_OFF = 0
K_OFF = RET_QK_WIDTH
V_OFF = 2 * RET_QK_WIDTH
G_OFF = V_OFF + RET_V_WIDTH
U_OFF = G_OFF + RET_V_WIDTH
GATE_OFF = U_OFF + POOL_WIDTH

POOL_HALO = 16
assert POOL_HALO >= max(POOL_WINDOWS) and POOL_HALO % 8 == 0

VMEM_LIMIT_BYTES = 56 * 1024 * 1024

BF16 = jnp.bfloat16
F32 = jnp.float32


def _params(n_axes):
    return pltpu.CompilerParams(
        dimension_semantics=("arbitrary",) * n_axes,
        vmem_limit_bytes=VMEM_LIMIT_BYTES,
    )


def _rope_kernel(pos_ref, freq_ref, cos_ref, sin_ref):
    ang = pos_ref[...] * freq_ref[...]
    lane = lax.broadcasted_iota(jnp.int32, ang.shape, 1)
    sin = jnp.sin(ang)
    cos_ref[...] = jnp.cos(ang)
    sin_ref[...] = jnp.where(lane < RET_QK_DIM // 2, -sin, sin)


def _rope_tables(positions):
    t = positions.size
    half = RET_QK_DIM // 2
    inv_freq = ROPE_BASE ** (-jnp.arange(half, dtype=F32) / half)
    freq = jnp.concatenate([inv_freq, inv_freq])[None, :]
    pos = jnp.broadcast_to(positions.reshape(t, 1).astype(F32), (t, RET_QK_DIM))
    tm = 1024
    spec = pl.BlockSpec((tm, RET_QK_DIM), lambda i: (i, 0))
    return pl.pallas_call(
        _rope_kernel,
        out_shape=(jax.ShapeDtypeStruct((t, RET_QK_DIM), F32),) * 2,
        grid=(t // tm,),
        in_specs=[spec, pl.BlockSpec((1, RET_QK_DIM), lambda i: (0, 0))],
        out_specs=(spec, spec),
        compiler_params=_params(1),
        name="rope_tables",
    )(pos, freq)


def _rmsnorm_to(x_ref, g_ref, out_ref):
    x = x_ref[...]
    ms = jnp.mean(x * x, axis=-1, keepdims=True)
    out_ref[...] = (x * lax.rsqrt(ms + NORM_EPS) * g_ref[...]).astype(out_ref.dtype)


def _rotary(t, cos, sin):
    outs = []
    for h in range(t.shape[1] // RET_QK_DIM):
        th = t[:, h * RET_QK_DIM:(h + 1) * RET_QK_DIM]
        outs.append(th * cos + pltpu.roll(th, RET_QK_DIM // 2, axis=1) * sin)
    return jnp.concatenate(outs, axis=1)


def _inproj_kernel(x_ref, g_ref, w_ref, cos_ref, sin_ref, pmix_ref, pscale_ref,
                   o_ref, xn_ref, halo_ref, *, tn, tiles_per_seq):
    i = pl.program_id(0)
    j = pl.program_id(1)
    tm = x_ref.shape[0]

    @pl.when(j == 0)
    def _():
        _rmsnorm_to(x_ref, g_ref, xn_ref)

    def project():
        return jnp.dot(xn_ref[...], w_ref[...], preferred_element_type=F32)

    @pl.when(j < V_OFF // tn)
    def _():
        scale = jnp.where(j == Q_OFF // tn, RET_QK_DIM ** -0.5, 1.0).astype(F32)
        rot = _rotary(project(), cos_ref[...] * scale, sin_ref[...] * scale)
        o_ref[...] = rot.astype(o_ref.dtype)

    @pl.when((j >= V_OFF // tn) & (j < G_OFF // tn))
    def _():
        o_ref[...] = project().astype(o_ref.dtype)

    @pl.when((j >= G_OFF // tn) & (j < U_OFF // tn))
    def _():
        acc = project()
        o_ref[...] = (acc * jax.nn.sigmoid(acc)).astype(o_ref.dtype)

    @pl.when(j == U_OFF // tn)
    def _():
        acc = project()
        seq_tile = i % tiles_per_seq

        @pl.when(seq_tile == 0)
        def _():
            halo_ref[...] = jnp.zeros_like(halo_ref)

        halo = halo_ref[...]
        halo_ref[...] = acc[tm - POOL_HALO:, :]
        row = lax.broadcasted_iota(jnp.int32, (tm, 1), 0) + seq_tile * tm
        pos1 = (row + 1).astype(F32)
        for gi, w in enumerate(POOL_WINDOWS):
            cols = slice(gi * POOL_GROUP_DIM, (gi + 1) * POOL_GROUP_DIM)
            u = acc[:, cols]
            s = jnp.concatenate([halo[:, cols], u], axis=0)
            shift = 1
            while shift < w:
                s = s + pltpu.roll(s, shift, axis=0)
                shift *= 2
            count = jnp.minimum(pos1, float(w))
            pooled = s[POOL_HALO:, :] / count - u
            mixed = jnp.dot(pooled.astype(BF16), pmix_ref[gi], preferred_element_type=F32)
            o_ref[:, cols] = (mixed * pscale_ref[:, cols]).astype(o_ref.dtype)

    @pl.when(j >= GATE_OFF // tn)
    def _():
        o_ref[...] = jax.nn.sigmoid(project()).astype(o_ref.dtype)


def _inproj(x, norm1, w_in, cos, sin, pool_mix, pool_scale, layer, seq):
    t = x.shape[0]
    tm, tn = 1024, 1024
    assert seq % tm == 0 and POOL_WIDTH == tn and RET_QK_WIDTH == tn
    kern = functools.partial(_inproj_kernel, tn=tn, tiles_per_seq=seq // tm)
    return pl.pallas_call(
        kern,
        out_shape=jax.ShapeDtypeStruct((t, IN_WIDTH), BF16),
        grid=(t // tm, IN_WIDTH // tn),
        in_specs=[
            pl.BlockSpec((tm, D_MODEL), lambda i, j: (i, 0)),
            pl.BlockSpec((None, 1, D_MODEL), lambda i, j: (layer, 0, 0)),
            pl.BlockSpec((None, D_MODEL, tn), lambda i, j: (layer, 0, j)),
            pl.BlockSpec((tm, RET_QK_DIM), lambda i, j: (i, 0)),
            pl.BlockSpec((tm, RET_QK_DIM), lambda i, j: (i, 0)),
            pl.BlockSpec((None, POOL_GROUPS, POOL_GROUP_DIM, POOL_GROUP_DIM),
                         lambda i, j: (layer, 0, 0, 0)),
            pl.BlockSpec((None, 1, POOL_WIDTH), lambda i, j: (layer, 0, 0)),
        ],
        out_specs=pl.BlockSpec((tm, tn), lambda i, j: (i, j)),
        scratch_shapes=[
            pltpu.VMEM((tm, D_MODEL), BF16),
            pltpu.VMEM((POOL_HALO, POOL_WIDTH), F32),
        ],
        compiler_params=_params(2),
        name="in_proj",
    )(x, norm1, w_in, cos, sin, pool_mix, pool_scale)


def _retention_tables():
    c = RET_CHUNK
    log_gamma = jnp.log(1.0 - 2.0 ** (-5.0 - jnp.arange(RET_HEADS, dtype=F32)))
    idx = jnp.arange(c, dtype=F32)
    diff = idx[:, None] - idx[None, :]
    decay_mask = jnp.where(diff[None] >= 0,
                           jnp.exp(jnp.maximum(diff, 0.0)[None] * log_gamma[:, None, None]),
                           0.0)
    xi = jnp.exp((idx + 1.0)[None, :] * log_gamma[:, None])
    zeta = jnp.exp((c - 1.0 - idx)[None, :] * log_gamma[:, None])
    chunk_decay = jnp.exp(c * log_gamma)
    xi = jnp.broadcast_to(xi[:, :, None], (RET_HEADS, c, RET_QK_DIM))
    zeta = jnp.broadcast_to(zeta[:, :, None], (RET_HEADS, c, RET_QK_DIM))
    chunk_decay = jnp.broadcast_to(chunk_decay[:, None, None], (RET_HEADS, 1, RET_V_DIM))
    return decay_mask, xi, zeta, chunk_decay


def _retention_kernel(q_ref, k_ref, v_ref, g_ref, dmask_ref, xi_ref, zeta_ref, cdecay_ref,
                      o_ref, state_ref):
    c = RET_CHUNK

    @pl.when(pl.program_id(1) == 0)
    def _():
        state_ref[...] = jnp.zeros_like(state_ref)

    heads = range(RET_HEADS)
    qk_cols = [slice(h * RET_QK_DIM, (h + 1) * RET_QK_DIM) for h in heads]
    v_cols = [slice(h * RET_V_DIM, (h + 1) * RET_V_DIM) for h in heads]
    for n in range(q_ref.shape[0] // c):
        rows = slice(n * c, (n + 1) * c)
        q = [q_ref[rows, qk_cols[h]] for h in heads]
        k = [k_ref[rows, qk_cols[h]] for h in heads]
        v = [v_ref[rows, v_cols[h]] for h in heads]
        scores = [lax.dot_general(q[h], k[h], (((1,), (1,)), ((), ())),
                                  preferred_element_type=F32) for h in heads]
        kz = [(k[h].astype(F32) * zeta_ref[h]).astype(BF16) for h in heads]
        kv = [lax.dot_general(kz[h], v[h], (((0,), (0,)), ((), ())),
                              preferred_element_type=F32) for h in heads]
        state = [state_ref[h] for h in heads]
        lhs = [jnp.concatenate([(scores[h] * dmask_ref[h]).astype(BF16),
                                (q[h].astype(F32) * xi_ref[h]).astype(BF16)], axis=1)
               for h in heads]
        rhs = [jnp.concatenate([v[h], state[h].astype(BF16)], axis=0) for h in heads]
        y = [jnp.dot(lhs[h], rhs[h], preferred_element_type=F32) for h in heads]
        for h in heads:
            state_ref[h] = state[h] * cdecay_ref[h] + kv[h]
            yn = y[h] * lax.rsqrt(jnp.mean(y[h] * y[h], axis=-1, keepdims=True) + NORM_EPS)
            o_ref[rows, v_cols[h]] = (yn * g_ref[rows, v_cols[h]].astype(F32)).astype(o_ref.dtype)


def _retention(proj, tables, batch, seq):
    t = proj.shape[0]
    ts = 512
    spb = seq // ts
    dmask, xi, zeta, cdecay = tables
    tok_blk = lambda width, off: pl.BlockSpec((ts, width), lambda b, s: (b * spb + s, off // width))
    full_blk = lambda a: pl.BlockSpec(a.shape, lambda b, s: (0,) * a.ndim)
    return pl.pallas_call(
        _retention_kernel,
        out_shape=jax.ShapeDtypeStruct((t, RET_V_WIDTH), BF16),
        grid=(batch, spb),
        in_specs=[
            tok_blk(RET_QK_WIDTH, Q_OFF), tok_blk(RET_QK_WIDTH, K_OFF),
            tok_blk(RET_V_WIDTH, V_OFF), tok_blk(RET_V_WIDTH, G_OFF),
            full_blk(dmask), full_blk(xi), full_blk(zeta), full_blk(cdecay),
        ],
        out_specs=tok_blk(RET_V_WIDTH, 0),
        scratch_shapes=[pltpu.VMEM((RET_HEADS, RET_QK_DIM, RET_V_DIM), F32)],
        compiler_params=_params(2),
        name="retention",
    )(proj, proj, proj, proj, dmask, xi, zeta, cdecay)


def _merge_kernel(yr_ref, yp_ref, g0_ref, g1_ref, wr_ref, wp_ref, o_ref):
    r = jnp.dot(yr_ref[...], wr_ref[...], preferred_element_type=F32)
    p = jnp.dot(yp_ref[...], wp_ref[...], preferred_element_type=F32)
    merged = g0_ref[...].astype(F32) * r + g1_ref[...].astype(F32) * p
    o_ref[...] = merged.astype(o_ref.dtype)


def _merge(y_ret, proj, w_ret_up, w_pool_up, layer):
    t = y_ret.shape[0]
    tm, tn = 1024, 1024
    assert POOL_WIDTH == tn
    return pl.pallas_call(
        _merge_kernel,
        out_shape=jax.ShapeDtypeStruct((t, D_MODEL), BF16),
        grid=(t // tm, D_MODEL // tn),
        in_specs=[
            pl.BlockSpec((tm, RET_V_WIDTH), lambda i, j: (i, 0)),
            pl.BlockSpec((tm, POOL_WIDTH), lambda i, j: (i, U_OFF // POOL_WIDTH)),
            pl.BlockSpec((tm, tn), lambda i, j: (i, GATE_OFF // tn + j)),
            pl.BlockSpec((tm, tn), lambda i, j: (i, (GATE_OFF + D_MODEL) // tn + j)),
            pl.BlockSpec((None, RET_V_WIDTH, tn), lambda i, j: (layer, 0, j)),
            pl.BlockSpec((None, POOL_WIDTH, tn), lambda i, j: (layer, 0, j)),
        ],
        out_specs=pl.BlockSpec((tm, tn), lambda i, j: (i, j)),
        compiler_params=_params(2),
        name="merge",
    )(y_ret, proj, proj, proj, w_ret_up, w_pool_up)


def _residual_matmul_kernel(x_ref, a_ref, w_ref, o_ref):
    o_ref[...] = x_ref[...] + jnp.dot(a_ref[...], w_ref[...], preferred_element_type=F32)


def _residual_matmul(x, a, w, layer, tm, tn, name):
    t, k = a.shape
    return pl.pallas_call(
        _residual_matmul_kernel,
        out_shape=jax.ShapeDtypeStruct(x.shape, x.dtype),
        grid=(t // tm, D_MODEL // tn),
        in_specs=[
            pl.BlockSpec((tm, tn), lambda i, j: (i, j)),
            pl.BlockSpec((tm, k), lambda i, j: (i, 0)),
            pl.BlockSpec((None, k, tn), lambda i, j: (layer, 0, j)),
        ],
        out_specs=pl.BlockSpec((tm, tn), lambda i, j: (i, j)),
        compiler_params=_params(2),
        name=name,
    )(x, a, w)


def _ffn_up_kernel(x_ref, g_ref, wg_ref, wu_ref, o_ref, xn_ref):
    @pl.when(pl.program_id(1) == 0)
    def _():
        _rmsnorm_to(x_ref, g_ref, xn_ref)

    xn = xn_ref[...]
    gate = jnp.dot(xn, wg_ref[...], preferred_element_type=F32)
    up = jnp.dot(xn, wu_ref[...], preferred_element_type=F32)
    o_ref[...] = (gate * jax.nn.sigmoid(gate) * up).astype(o_ref.dtype)


def _ffn_up(x, norm2, w_gu, layer):
    t = x.shape[0]
    d_ff = w_gu.shape[2] // 2
    tm, tn = 1024, 512
    assert d_ff % tn == 0
    return pl.pallas_call(
        _ffn_up_kernel,
        out_shape=jax.ShapeDtypeStruct((t, d_ff), BF16),
        grid=(t // tm, d_ff // tn),
        in_specs=[
            pl.BlockSpec((tm, D_MODEL), lambda i, j: (i, 0)),
            pl.BlockSpec((None, 1, D_MODEL), lambda i, j: (layer, 0, 0)),
            pl.BlockSpec((None, D_MODEL, tn), lambda i, j: (layer, 0, j)),
            pl.BlockSpec((None, D_MODEL, tn), lambda i, j: (layer, 0, d_ff // tn + j)),
        ],
        out_specs=pl.BlockSpec((tm, tn), lambda i, j: (i, j)),
        scratch_shapes=[pltpu.VMEM((tm, D_MODEL), BF16)],
        compiler_params=_params(2),
        name="ffn_up",
    )(x, norm2, w_gu, w_gu)


def _final_norm_kernel(x_ref, g_ref, o_ref):
    _rmsnorm_to(x_ref, g_ref, o_ref)


def _final_norm(x, norm_f):
    t = x.shape[0]
    tm = 512
    return pl.pallas_call(
        _final_norm_kernel,
        out_shape=jax.ShapeDtypeStruct(x.shape, x.dtype),
        grid=(t // tm,),
        in_specs=[
            pl.BlockSpec((tm, D_MODEL), lambda i: (i, 0)),
            pl.BlockSpec((1, D_MODEL), lambda i: (0, 0)),
        ],
        out_specs=pl.BlockSpec((tm, D_MODEL), lambda i: (i, 0)),
        compiler_params=_params(1),
        name="final_norm",
    )(x, norm_f)


def kernel(x, positions, norm1, w_in, pool_mix, pool_scale, w_ret_up, w_pool_up, w_o,
           norm2, w_gu, w_down, norm_f):
    batch, seq, _ = x.shape
    depth = w_in.shape[0]
    t = batch * seq
    xt = x.reshape(t, D_MODEL)

    w_in_b = w_in.astype(BF16)
    pool_mix_b = pool_mix.astype(BF16)
    w_ret_up_b = w_ret_up.astype(BF16)
    w_pool_up_b = w_pool_up.astype(BF16)
    w_o_b = w_o.astype(BF16)
    w_gu_b = w_gu.astype(BF16)
    w_down_b = w_down.astype(BF16)
    norm1_r = norm1.reshape(depth, 1, D_MODEL)
    norm2_r = norm2.reshape(depth, 1, D_MODEL)
    pool_scale_r = pool_scale.reshape(depth, 1, POOL_WIDTH)

    cos, sin = _rope_tables(positions)
    tables = _retention_tables()

    for layer in range(depth):
        proj = _inproj(xt, norm1_r, w_in_b, cos, sin, pool_mix_b, pool_scale_r, layer, seq)
        y_ret = _retention(proj, tables, batch, seq)
        merged = _merge(y_ret, proj, w_ret_up_b, w_pool_up_b, layer)
        xt = _residual_matmul(xt, merged, w_o_b, layer, 1024, 1024, "out_proj")
        hidden = _ffn_up(xt, norm2_r, w_gu_b, layer)
        xt = _residual_matmul(xt, hidden, w_down_b, layer, 1024, 512, "ffn_down")
    out = _final_norm(xt, norm_f.reshape(1, D_MODEL))
    return out.reshape(batch, seq, D_MODEL)
```

```python
import functools
from typing import NamedTuple

import jax
import jax.numpy as jnp
from jax import lax
from jax.experimental import pallas as pl
from jax.experimental.pallas import tpu as pltpu

D_MODEL = 2048
RET_HEADS = 8
RET_QK_DIM = 128
RET_V_DIM = 256
RET_QK_WIDTH = RET_HEADS * RET_QK_DIM
RET_V_WIDTH = RET_HEADS * RET_V_DIM
RET_CHUNK = 128
ROPE_BASE = 10000.0
POOL_WINDOWS = (2, 4, 8, 16)
POOL_GROUPS = len(POOL_WINDOWS)
POOL_GROUP_DIM = D_MODEL // 8
POOL_WIDTH = POOL_GROUPS * POOL_GROUP_DIM
N_BRANCHES = 2
IN_WIDTH = 2 * RET_QK_WIDTH + 2 * RET_V_WIDTH + POOL_WIDTH + N_BRANCHES * D_MODEL
NORM_EPS = 1e-6

Q_OFF = 0
K_OFF = RET_QK_WIDTH
V_OFF = 2 * RET_QK_WIDTH
G_OFF = V_OFF + RET_V_WIDTH
U_OFF = G_OFF + RET_V_WIDTH
GATE_OFF = U_OFF + POOL_WIDTH

POOL_HALO = 16
assert POOL_HALO >= max(POOL_WINDOWS) and POOL_HALO % 8 == 0

VMEM_LIMIT_BYTES = 56 * 1024 * 1024

BF16 = jnp.bfloat16
F32 = jnp.float32


def _params(n_axes):
    return pltpu.CompilerParams(
        dimension_semantics=("arbitrary",) * n_axes,
        vmem_limit_bytes=VMEM_LIMIT_BYTES,
    )


def _rope_kernel(pos_ref, freq_ref, cos_ref, sin_ref):
    ang = pos_ref[...] * freq_ref[...]
    lane = lax.broadcasted_iota(jnp.int32, ang.shape, 1)
    sin = jnp.sin(ang)
    cos_ref[...] = jnp.cos(ang)
    sin_ref[...] = jnp.where(lane < RET_QK_DIM // 2, -sin, sin)


def _rope_tables(positions):
    t = positions.size
    half = RET_QK_DIM // 2
    inv_freq = ROPE_BASE ** (-jnp.arange(half, dtype=F32) / half)
    freq = jnp.concatenate([inv_freq, inv_freq])[None, :]
    pos = jnp.broadcast_to(positions.reshape(t, 1).astype(F32), (t, RET_QK_DIM))
    tm = 1024
    spec = pl.BlockSpec((tm, RET_QK_DIM), lambda i: (i, 0))
    return pl.pallas_call(
        _rope_kernel,
        out_shape=(jax.ShapeDtypeStruct((t, RET_QK_DIM), F32),) * 2,
        grid=(t // tm,),
        in_specs=[spec, pl.BlockSpec((1, RET_QK_DIM), lambda i: (0, 0))],
        out_specs=(spec, spec),
        compiler_params=_params(1),
        name="rope_tables",
    )(pos, freq)


BF16_SUBLANES = 16


class _CastStream(NamedTuple):
    stacked: jax.Array
    in_spec: pl.BlockSpec
    out_spec: pl.BlockSpec
    out_shape: jax.ShapeDtypeStruct


def _cast_stream(stacked, layer, grid):
    _, k, n = stacked.shape
    n_steps = grid[0] * grid[1]
    rows = BF16_SUBLANES
    while k % rows or k // rows > n_steps:
        rows += BF16_SUBLANES
    last = k // rows - 1
    block = lambda i, j: jnp.minimum(i * grid[1] + j, last)
    return _CastStream(
        stacked,
        pl.BlockSpec((None, rows, n), lambda i, j: (layer, block(i, j), 0)),
        pl.BlockSpec((rows, n), lambda i, j: (block(i, j), 0)),
        jax.ShapeDtypeStruct((k, n), BF16),
    )


def _run_casts(src_refs, dst_refs):
    for src, dst in zip(src_refs, dst_refs, strict=True):
        dst[...] = src[...].astype(dst.dtype)


def _rmsnorm_to(x_ref, g_ref, out_ref):
    x = x_ref[...]
    ms = jnp.mean(x * x, axis=-1, keepdims=True)
    out_ref[...] = (x * lax.rsqrt(ms + NORM_EPS) * g_ref[...]).astype(out_ref.dtype)


def _rotary(t, cos, sin):
    outs = []
    for h in range(t.shape[1] // RET_QK_DIM):
        th = t[:, h * RET_QK_DIM:(h + 1) * RET_QK_DIM]
        outs.append(th * cos + pltpu.roll(th, RET_QK_DIM // 2, axis=1) * sin)
    return jnp.concatenate(outs, axis=1)


def _inproj_kernel(*refs, tn, tiles_per_seq, n_cast):
    x_ref, g_ref, w_ref, cos_ref, sin_ref, pmix_ref, pscale_ref = refs[:7]
    cast_src = refs[7:7 + n_cast]
    o_ref = refs[7 + n_cast]
    cast_dst = refs[8 + n_cast:8 + 2 * n_cast]
    xn_ref, halo_ref = refs[8 + 2 * n_cast:]
    i = pl.program_id(0)
    j = pl.program_id(1)
    tm = x_ref.shape[0]

    @pl.when(j == 0)
    def _():
        _rmsnorm_to(x_ref, g_ref, xn_ref)

    def project():
        _run_casts(cast_src, cast_dst)
        return jnp.dot(xn_ref[...], w_ref[...], preferred_element_type=F32)

    @pl.when(j < V_OFF // tn)
    def _():
        scale = jnp.where(j == Q_OFF // tn, RET_QK_DIM ** -0.5, 1.0).astype(F32)
        rot = _rotary(project(), cos_ref[...] * scale, sin_ref[...] * scale)
        o_ref[...] = rot.astype(o_ref.dtype)

    @pl.when((j >= V_OFF // tn) & (j < G_OFF // tn))
    def _():
        o_ref[...] = project().astype(o_ref.dtype)

    @pl.when((j >= G_OFF // tn) & (j < U_OFF // tn))
    def _():
        acc = project()
        o_ref[...] = (acc * jax.nn.sigmoid(acc)).astype(o_ref.dtype)

    @pl.when(j == U_OFF // tn)
    def _():
        acc = project()
        seq_tile = i % tiles_per_seq

        @pl.when(seq_tile == 0)
        def _():
            halo_ref[...] = jnp.zeros_like(halo_ref)

        halo = halo_ref[...]
        halo_ref[...] = acc[tm - POOL_HALO:, :]
        row = lax.broadcasted_iota(jnp.int32, (tm, 1), 0) + seq_tile * tm
        pos1 = (row + 1).astype(F32)
        for gi, w in enumerate(POOL_WINDOWS):
            cols = slice(gi * POOL_GROUP_DIM, (gi + 1) * POOL_GROUP_DIM)
            u = acc[:, cols]
            s = jnp.concatenate([halo[:, cols], u], axis=0)
            shift = 1
            while shift < w:
                s = s + pltpu.roll(s, shift, axis=0)
                shift *= 2
            count = jnp.minimum(pos1, float(w))
            pooled = s[POOL_HALO:, :] / count - u
            mixed = jnp.dot(pooled.astype(BF16), pmix_ref[gi], preferred_element_type=F32)
            o_ref[:, cols] = (mixed * pscale_ref[:, cols]).astype(o_ref.dtype)

    @pl.when(j >= GATE_OFF // tn)
    def _():
        o_ref[...] = jax.nn.sigmoid(project()).astype(o_ref.dtype)


def _inproj(x, norm1, w_in_b, cos, sin, pool_mix, pool_scale, layer, seq, cast_weights):
    t = x.shape[0]
    tm, tn = 1024, 1024
    assert seq % tm == 0 and POOL_WIDTH == tn and RET_QK_WIDTH == tn
    grid = (t // tm, IN_WIDTH // tn)
    casts = [_cast_stream(w, layer, grid) for w in cast_weights]
    kern = functools.partial(_inproj_kernel, tn=tn, tiles_per_seq=seq // tm, n_cast=len(casts))
    return pl.pallas_call(
        kern,
        out_shape=[jax.ShapeDtypeStruct((t, IN_WIDTH), BF16)] + [c.out_shape for c in casts],
        grid=grid,
        in_specs=[
            pl.BlockSpec((tm, D_MODEL), lambda i, j: (i, 0)),
            pl.BlockSpec((None, 1, D_MODEL), lambda i, j: (layer, 0, 0)),
            pl.BlockSpec((D_MODEL, tn), lambda i, j: (0, j)),
            pl.BlockSpec((tm, RET_QK_DIM), lambda i, j: (i, 0)),
            pl.BlockSpec((tm, RET_QK_DIM), lambda i, j: (i, 0)),
            pl.BlockSpec((None, POOL_GROUPS, POOL_GROUP_DIM, POOL_GROUP_DIM),
                         lambda i, j: (layer, 0, 0, 0)),
            pl.BlockSpec((None, 1, POOL_WIDTH), lambda i, j: (layer, 0, 0)),
        ] + [c.in_spec for c in casts],
        out_specs=[pl.BlockSpec((tm, tn), lambda i, j: (i, j))] + [c.out_spec for c in casts],
        scratch_shapes=[
            pltpu.VMEM((tm, D_MODEL), BF16),
            pltpu.VMEM((POOL_HALO, POOL_WIDTH), F32),
        ],
        compiler_params=_params(2),
        name="in_proj",
    )(x, norm1, w_in_b, cos, sin, pool_mix, pool_scale, *[c.stacked for c in casts])


def _retention_tables():
    c = RET_CHUNK
    log_gamma = jnp.log(1.0 - 2.0 ** (-5.0 - jnp.arange(RET_HEADS, dtype=F32)))
    idx = jnp.arange(c, dtype=F32)
    diff = idx[:, None] - idx[None, :]
    decay_mask = jnp.where(diff[None] >= 0,
                           jnp.exp(jnp.maximum(diff, 0.0)[None] * log_gamma[:, None, None]),
                           0.0)
    xi = jnp.exp((idx + 1.0)[None, :] * log_gamma[:, None])
    zeta = jnp.exp((c - 1.0 - idx)[None, :] * log_gamma[:, None])
    chunk_decay = jnp.exp(c * log_gamma)
    xi = jnp.broadcast_to(xi[:, :, None], (RET_HEADS, c, RET_QK_DIM))
    zeta = jnp.broadcast_to(zeta[:, :, None], (RET_HEADS, c, RET_QK_DIM))
    chunk_decay = jnp.broadcast_to(chunk_decay[:, None, None], (RET_HEADS, 1, RET_V_DIM))
    return decay_mask, xi, zeta, chunk_decay


def _retention_kernel(q_ref, k_ref, v_ref, g_ref, dmask_ref, xi_ref, zeta_ref, cdecay_ref,
                      o_ref, state_ref):
    c = RET_CHUNK

    @pl.when(pl.program_id(1) == 0)
    def _():
        state_ref[...] = jnp.zeros_like(state_ref)

    heads = range(RET_HEADS)
    qk_cols = [slice(h * RET_QK_DIM, (h + 1) * RET_QK_DIM) for h in heads]
    v_cols = [slice(h * RET_V_DIM, (h + 1) * RET_V_DIM) for h in heads]
    for n in range(q_ref.shape[0] // c):
        rows = slice(n * c, (n + 1) * c)
        q = [q_ref[rows, qk_cols[h]] for h in heads]
        k = [k_ref[rows, qk_cols[h]] for h in heads]
        v = [v_ref[rows, v_cols[h]] for h in heads]
        scores = [lax.dot_general(q[h], k[h], (((1,), (1,)), ((), ())),
                                  preferred_element_type=F32) for h in heads]
        kz = [(k[h].astype(F32) * zeta_ref[h]).astype(BF16) for h in heads]
        kv = [lax.dot_general(kz[h], v[h], (((0,), (0,)), ((), ())),
                              preferred_element_type=F32) for h in heads]
        state = [state_ref[h] for h in heads]
        lhs = [jnp.concatenate([(scores[h] * dmask_ref[h]).astype(BF16),
                                (q[h].astype(F32) * xi_ref[h]).astype(BF16)], axis=1)
               for h in heads]
        rhs = [jnp.concatenate([v[h], state[h].astype(BF16)], axis=0) for h in heads]
        y = [jnp.dot(lhs[h], rhs[h], preferred_element_type=F32) for h in heads]
        for h in heads:
            state_ref[h] = state[h] * cdecay_ref[h] + kv[h]
            yn = y[h] * lax.rsqrt(jnp.mean(y[h] * y[h], axis=-1, keepdims=True) + NORM_EPS)
            o_ref[rows, v_cols[h]] = (yn * g_ref[rows, v_cols[h]].astype(F32)).astype(o_ref.dtype)


def _retention(proj, tables, batch, seq):
    t = proj.shape[0]
    ts = 512
    spb = seq // ts
    dmask, xi, zeta, cdecay = tables
    tok_blk = lambda width, off: pl.BlockSpec((ts, width), lambda b, s: (b * spb + s, off // width))
    full_blk = lambda a: pl.BlockSpec(a.shape, lambda b, s: (0,) * a.ndim)
    return pl.pallas_call(
        _retention_kernel,
        out_shape=jax.ShapeDtypeStruct((t, RET_V_WIDTH), BF16),
        grid=(batch, spb),
        in_specs=[
            tok_blk(RET_QK_WIDTH, Q_OFF), tok_blk(RET_QK_WIDTH, K_OFF),
            tok_blk(RET_V_WIDTH, V_OFF), tok_blk(RET_V_WIDTH, G_OFF),
            full_blk(dmask), full_blk(xi), full_blk(zeta), full_blk(cdecay),
        ],
        out_specs=tok_blk(RET_V_WIDTH, 0),
        scratch_shapes=[pltpu.VMEM((RET_HEADS, RET_QK_DIM, RET_V_DIM), F32)],
        compiler_params=_params(2),
        name="retention",
    )(proj, proj, proj, proj, dmask, xi, zeta, cdecay)


def _merge_kernel(yr_ref, yp_ref, g0_ref, g1_ref, wr_ref, wp_ref, o_ref):
    r = jnp.dot(yr_ref[...], wr_ref[...], preferred_element_type=F32)
    p = jnp.dot(yp_ref[...], wp_ref[...], preferred_element_type=F32)
    merged = g0_ref[...].astype(F32) * r + g1_ref[...].astype(F32) * p
    o_ref[...] = merged.astype(o_ref.dtype)


def _merge(y_ret, proj, w_ret_up_b, w_pool_up_b):
    t = y_ret.shape[0]
    tm, tn = 1024, 1024
    assert POOL_WIDTH == tn
    return pl.pallas_call(
        _merge_kernel,
        out_shape=jax.ShapeDtypeStruct((t, D_MODEL), BF16),
        grid=(t // tm, D_MODEL // tn),
        in_specs=[
            pl.BlockSpec((tm, RET_V_WIDTH), lambda i, j: (i, 0)),
            pl.BlockSpec((tm, POOL_WIDTH), lambda i, j: (i, U_OFF // POOL_WIDTH)),
            pl.BlockSpec((tm, tn), lambda i, j: (i, GATE_OFF // tn + j)),
            pl.BlockSpec((tm, tn), lambda i, j: (i, (GATE_OFF + D_MODEL) // tn + j)),
            pl.BlockSpec((RET_V_WIDTH, tn), lambda i, j: (0, j)),
            pl.BlockSpec((POOL_WIDTH, tn), lambda i, j: (0, j)),
        ],
        out_specs=pl.BlockSpec((tm, tn), lambda i, j: (i, j)),
        compiler_params=_params(2),
        name="merge",
    )(y_ret, proj, proj, proj, w_ret_up_b, w_pool_up_b)


def _residual_matmul_kernel(x_ref, a_ref, w_ref, o_ref):
    o_ref[...] = x_ref[...] + jnp.dot(a_ref[...], w_ref[...], preferred_element_type=F32)


def _residual_matmul(x, a, w_b, tm, tn, name):
    t, k = a.shape
    return pl.pallas_call(
        _residual_matmul_kernel,
        out_shape=jax.ShapeDtypeStruct(x.shape, x.dtype),
        grid=(t // tm, D_MODEL // tn),
        in_specs=[
            pl.BlockSpec((tm, tn), lambda i, j: (i, j)),
            pl.BlockSpec((tm, k), lambda i, j: (i, 0)),
            pl.BlockSpec((k, tn), lambda i, j: (0, j)),
        ],
        out_specs=pl.BlockSpec((tm, tn), lambda i, j: (i, j)),
        compiler_params=_params(2),
        name=name,
    )(x, a, w_b)


def _ffn_up_kernel(*refs, n_cast):
    x_ref, g_ref, wg_ref, wu_ref = refs[:4]
    cast_src = refs[4:4 + n_cast]
    o_ref = refs[4 + n_cast]
    cast_dst = refs[5 + n_cast:5 + 2 * n_cast]
    xn_ref = refs[5 + 2 * n_cast]

    @pl.when(pl.program_id(1) == 0)
    def _():
        _rmsnorm_to(x_ref, g_ref, xn_ref)

    _run_casts(cast_src, cast_dst)
    xn = xn_ref[...]
    gate = jnp.dot(xn, wg_ref[...], preferred_element_type=F32)
    up = jnp.dot(xn, wu_ref[...], preferred_element_type=F32)
    o_ref[...] = (gate * jax.nn.sigmoid(gate) * up).astype(o_ref.dtype)


def _ffn_up(x, norm2, w_gu_b, layer, cast_weights):
    t = x.shape[0]
    d_ff = w_gu_b.shape[1] // 2
    tm, tn = 1024, 512
    assert d_ff % tn == 0
    grid = (t // tm, d_ff // tn)
    casts = [_cast_stream(w, layer + 1, grid) for w in cast_weights]
    return pl.pallas_call(
        functools.partial(_ffn_up_kernel, n_cast=len(casts)),
        out_shape=[jax.ShapeDtypeStruct((t, d_ff), BF16)] + [c.out_shape for c in casts],
        grid=grid,
        in_specs=[
            pl.BlockSpec((tm, D_MODEL), lambda i, j: (i, 0)),
            pl.BlockSpec((None, 1, D_MODEL), lambda i, j: (layer, 0, 0)),
            pl.BlockSpec((D_MODEL, tn), lambda i, j: (0, j)),
            pl.BlockSpec((D_MODEL, tn), lambda i, j: (0, d_ff // tn + j)),
        ] + [c.in_spec for c in casts],
        out_specs=[pl.BlockSpec((tm, tn), lambda i, j: (i, j))] + [c.out_spec for c in casts],
        scratch_shapes=[pltpu.VMEM((tm, D_MODEL), BF16)],
        compiler_params=_params(2),
        name="ffn_up",
    )(x, norm2, w_gu_b, w_gu_b, *[c.stacked for c in casts])


def _final_norm_kernel(x_ref, g_ref, o_ref):
    _rmsnorm_to(x_ref, g_ref, o_ref)


def _final_norm(x, norm_f):
    t = x.shape[0]
    tm = 512
    return pl.pallas_call(
        _final_norm_kernel,
        out_shape=jax.ShapeDtypeStruct(x.shape, x.dtype),
        grid=(t // tm,),
        in_specs=[
            pl.BlockSpec((tm, D_MODEL), lambda i: (i, 0)),
            pl.BlockSpec((1, D_MODEL), lambda i: (0, 0)),
        ],
        out_specs=pl.BlockSpec((tm, D_MODEL), lambda i: (i, 0)),
        compiler_params=_params(1),
        name="final_norm",
    )(x, norm_f)


def kernel(x, positions, norm1, w_in, pool_mix, pool_scale, w_ret_up, w_pool_up, w_o,
           norm2, w_gu, w_down, norm_f):
    batch, seq, _ = x.shape
    depth = w_in.shape[0]
    t = batch * seq
    xt = x.reshape(t, D_MODEL)

    pool_mix_b = pool_mix.astype(BF16)
    norm1_r = norm1.reshape(depth, 1, D_MODEL)
    norm2_r = norm2.reshape(depth, 1, D_MODEL)
    pool_scale_r = pool_scale.reshape(depth, 1, POOL_WIDTH)

    cos, sin = _rope_tables(positions)
    tables = _retention_tables()

    w_in_b, w_ret_up_b, w_pool_up_b, w_o_b = (
        w[0].astype(BF16) for w in (w_in, w_ret_up, w_pool_up, w_o))
    for layer in range(depth):
        proj, w_gu_b, w_down_b = _inproj(
            xt, norm1_r, w_in_b, cos, sin, pool_mix_b, pool_scale_r, layer, seq, (w_gu, w_down))
        y_ret = _retention(proj, tables, batch, seq)
        merged = _merge(y_ret, proj, w_ret_up_b, w_pool_up_b)
        xt = _residual_matmul(xt, merged, w_o_b, 1024, 1024, "out_proj")
        next_weights = (w_in, w_ret_up, w_pool_up, w_o) if layer + 1 < depth else ()
        hidden, *next_b = _ffn_up(xt, norm2_r, w_gu_b, layer, next_weights)
        if next_b:
            w_in_b, w_ret_up_b, w_pool_up_b, w_o_b = next_b
        xt = _residual_matmul(xt, hidden, w_down_b, 1024, 512, "ffn_down")
    out = _final_norm(xt, norm_f.reshape(1, D_MODEL))
    return out.reshape(batch, seq, D_MODEL)
```

```python
import functools
from typing import NamedTuple

import jax
import jax.numpy as jnp
from jax import lax
from jax.experimental import pallas as pl
from jax.experimental.pallas import tpu as pltpu

D_MODEL = 2048
RET_HEADS = 8
RET_QK_DIM = 128
RET_V_DIM = 256
RET_QK_WIDTH = RET_HEADS * RET_QK_DIM
RET_V_WIDTH = RET_HEADS * RET_V_DIM
RET_CHUNK = 128
ROPE_BASE = 10000.0
POOL_WINDOWS = (2, 4, 8, 16)
POOL_GROUPS = len(POOL_WINDOWS)
POOL_GROUP_DIM = D_MODEL // 8
POOL_WIDTH = POOL_GROUPS * POOL_GROUP_DIM
N_BRANCHES = 2
IN_WIDTH = 2 * RET_QK_WIDTH + 2 * RET_V_WIDTH + POOL_WIDTH + N_BRANCHES * D_MODEL
NORM_EPS = 1e-6

Q_OFF = 0
K_OFF = RET_QK_WIDTH
V_OFF = 2 * RET_QK_WIDTH
G_OFF = V_OFF + RET_V_WIDTH
U_OFF = G_OFF + RET_V_WIDTH
GATE_OFF = U_OFF + POOL_WIDTH

POOL_HALO = 16
assert POOL_HALO >= max(POOL_WINDOWS) and POOL_HALO % 8 == 0

VMEM_LIMIT_BYTES = 56 * 1024 * 1024
MXU_COLS = 256
assert POOL_GROUP_DIM == MXU_COLS

BF16 = jnp.bfloat16
F32 = jnp.float32


def _params(n_axes):
    return pltpu.CompilerParams(
        dimension_semantics=("arbitrary",) * n_axes,
        vmem_limit_bytes=VMEM_LIMIT_BYTES,
    )


def _rope_kernel(pos_ref, freq_ref, cos_ref, sin_ref):
    ang = pos_ref[...] * freq_ref[...]
    lane = lax.broadcasted_iota(jnp.int32, ang.shape, 1)
    sin = jnp.sin(ang)
    cos_ref[...] = jnp.cos(ang)
    sin_ref[...] = jnp.where(lane < RET_QK_DIM // 2, -sin, sin)


def _rope_tables(positions):
    t = positions.size
    half = RET_QK_DIM // 2
    inv_freq = ROPE_BASE ** (-jnp.arange(half, dtype=F32) / half)
    freq = jnp.concatenate([inv_freq, inv_freq])[None, :]
    pos = jnp.broadcast_to(positions.reshape(t, 1).astype(F32), (t, RET_QK_DIM))
    tm = 1024
    spec = pl.BlockSpec((tm, RET_QK_DIM), lambda i: (i, 0))
    return pl.pallas_call(
        _rope_kernel,
        out_shape=(jax.ShapeDtypeStruct((t, RET_QK_DIM), F32),) * 2,
        grid=(t // tm,),
        in_specs=[spec, pl.BlockSpec((1, RET_QK_DIM), lambda i: (0, 0))],
        out_specs=(spec, spec),
        compiler_params=_params(1),
        name="rope_tables",
    )(pos, freq)


BF16_SUBLANES = 16


class _CastStream(NamedTuple):
    stacked: jax.Array
    in_spec: pl.BlockSpec
    out_specs: tuple
    out_shapes: tuple


def _cast_stream(stacked, layer, grid, n_split=1):
    _, k, n = stacked.shape
    n_steps = grid[0] * grid[1]
    rows = BF16_SUBLANES
    while k % rows or k // rows > n_steps:
        rows += BF16_SUBLANES
    last = k // rows - 1
    block = lambda i, j: jnp.minimum(i * grid[1] + j, last)
    return _CastStream(
        stacked,
        pl.BlockSpec((None, rows, n), lambda i, j: (layer, block(i, j), 0)),
        (pl.BlockSpec((rows, n // n_split), lambda i, j: (block(i, j), 0)),) * n_split,
        (jax.ShapeDtypeStruct((k, n // n_split), BF16),) * n_split,
    )


def _cast_out_specs(casts):
    return [s for c in casts for s in c.out_specs]


def _cast_out_shapes(casts):
    return [s for c in casts for s in c.out_shapes]


def _run_casts(src_refs, dst_refs):
    d = 0
    for src in src_refs:
        col = 0
        while col < src.shape[1]:
            dst = dst_refs[d]
            dst[...] = src[:, col:col + dst.shape[1]].astype(dst.dtype)
            col += dst.shape[1]
            d += 1
    assert d == len(dst_refs)


def _sigmoid(x):
    return 0.5 * jnp.tanh(0.5 * x) + 0.5


def _silu(x):
    h = 0.5 * x
    return h + h * jnp.tanh(h)


def _rotary(t, cos, sin):
    outs = []
    for h in range(t.shape[1] // RET_QK_DIM):
        th = t[:, h * RET_QK_DIM:(h + 1) * RET_QK_DIM]
        outs.append(th * cos + pltpu.roll(th, RET_QK_DIM // 2, axis=1) * sin)
    return jnp.concatenate(outs, axis=1)


def _rmsnorm_rows(x_ref, gain_ref, out_ref):
    n_blocks = x_ref.shape[1] // MXU_COLS
    col = lambda c: slice(c * MXU_COLS, (c + 1) * MXU_COLS)
    sq = None
    for c in range(n_blocks):
        xc = x_ref[:, col(c)]
        sq = xc * xc if sq is None else sq + xc * xc
    scale = lax.rsqrt(jnp.sum(sq, axis=-1, keepdims=True) / x_ref.shape[1] + NORM_EPS)
    for c in range(n_blocks):
        out_ref[:, col(c)] = (x_ref[:, col(c)] * scale * gain_ref[:, col(c)]).astype(out_ref.dtype)


def _norm_kernel(x_ref, g_ref, o_ref):
    _rmsnorm_rows(x_ref, g_ref, o_ref)


def _norm(x, gain, out_dtype):
    t = x.shape[0]
    tm = 512
    return pl.pallas_call(
        _norm_kernel,
        out_shape=jax.ShapeDtypeStruct(x.shape, out_dtype),
        grid=(t // tm,),
        in_specs=[
            pl.BlockSpec((tm, D_MODEL), lambda i: (i, 0)),
            pl.BlockSpec((1, D_MODEL), lambda i: (0, 0)),
        ],
        out_specs=pl.BlockSpec((tm, D_MODEL), lambda i: (i, 0)),
        compiler_params=_params(1),
        name="input_norm",
    )(x, gain)


SECTION = 1024
SECTIONS_PER_STEP = 2
N_SECTIONS = IN_WIDTH // SECTION
assert all(off % SECTION == 0 for off in (K_OFF, V_OFF, G_OFF, U_OFF, GATE_OFF))
assert RET_QK_WIDTH == SECTION and POOL_WIDTH == SECTION


def _inproj_kernel(*refs, tiles_per_seq, n_cast_src, n_cast_dst):
    xn_ref, w_ref, cos_ref, sin_ref, pmix_ref, pscale_ref = refs[:6]
    cast_src = refs[6:6 + n_cast_src]
    o_ref = refs[6 + n_cast_src]
    cast_dst = refs[7 + n_cast_src:7 + n_cast_src + n_cast_dst]
    halo_ref = refs[7 + n_cast_src + n_cast_dst]
    i = pl.program_id(0)
    j = pl.program_id(1)
    tm = xn_ref.shape[0]

    def project(cols):
        return jnp.dot(xn_ref[...], w_ref[:, cols], preferred_element_type=F32)

    def rotary_section(cols, scale):
        rot = _rotary(project(cols), cos_ref[...] * scale, sin_ref[...] * scale)
        o_ref[:, cols] = rot.astype(o_ref.dtype)

    def plain_section(cols):
        o_ref[:, cols] = project(cols).astype(o_ref.dtype)

    def silu_section(cols):
        o_ref[:, cols] = _silu(project(cols)).astype(o_ref.dtype)

    def sigmoid_section(cols):
        o_ref[:, cols] = _sigmoid(project(cols)).astype(o_ref.dtype)

    def pool_section(cols):
        seq_tile = i % tiles_per_seq

        @pl.when(seq_tile == 0)
        def _():
            halo_ref[...] = jnp.zeros_like(halo_ref)

        row = lax.broadcasted_iota(jnp.int32, (tm, 1), 0) + seq_tile * tm
        pos1 = (row + 1).astype(F32)
        xn = xn_ref[...]
        group = lambda gi: slice(gi * POOL_GROUP_DIM, (gi + 1) * POOL_GROUP_DIM)
        shifted = lambda gi: slice(cols.start + gi * POOL_GROUP_DIM,
                                   cols.start + (gi + 1) * POOL_GROUP_DIM)

        def pool_and_mix(gi, u):
            s = jnp.concatenate([halo_ref[:, group(gi)], u], axis=0)
            halo_ref[:, group(gi)] = u[tm - POOL_HALO:, :]
            shift = 1
            while shift < POOL_WINDOWS[gi]:
                s = s + pltpu.roll(s, shift, axis=0)
                shift *= 2
            count = jnp.minimum(pos1, float(POOL_WINDOWS[gi]))
            pooled = s[POOL_HALO:, :] / count - u
            mixed = jnp.dot(pooled.astype(BF16), pmix_ref[gi], preferred_element_type=F32)
            o_ref[:, shifted(gi)] = (mixed * pscale_ref[:, group(gi)]).astype(o_ref.dtype)

        u_prev = None
        for gi in range(POOL_GROUPS):
            u = jnp.dot(xn, w_ref[:, shifted(gi)], preferred_element_type=F32)
            if u_prev is not None:
                pool_and_mix(gi - 1, u_prev)
            u_prev = u
        pool_and_mix(POOL_GROUPS - 1, u_prev)

    def section_fn(index):
        off = index * SECTION
        if off == Q_OFF:
            return functools.partial(rotary_section, scale=RET_QK_DIM ** -0.5)
        if off == K_OFF:
            return functools.partial(rotary_section, scale=1.0)
        if off < G_OFF:
            return plain_section
        if off < U_OFF:
            return silu_section
        if off == U_OFF:
            return pool_section
        return sigmoid_section

    for step in range(pl.cdiv(N_SECTIONS, SECTIONS_PER_STEP)):
        @pl.when(j == step)
        def _(step=step):
            _run_casts(cast_src, cast_dst)
            for half in range(SECTIONS_PER_STEP):
                index = step * SECTIONS_PER_STEP + half
                if index < N_SECTIONS:
                    section_fn(index)(slice(half * SECTION, (half + 1) * SECTION))


def _inproj(xn, w_in_b, cos, sin, pool_mix, pool_scale, layer, seq, casts_of):
    t = xn.shape[0]
    tm, tn = 1024, SECTION * SECTIONS_PER_STEP
    assert seq % tm == 0
    grid = (t // tm, pl.cdiv(IN_WIDTH, tn))
    casts = casts_of(grid)
    kern = functools.partial(_inproj_kernel, tiles_per_seq=seq // tm, n_cast_src=len(casts),
                             n_cast_dst=len(_cast_out_specs(casts)))
    return pl.pallas_call(
        kern,
        out_shape=[jax.ShapeDtypeStruct((t, IN_WIDTH), BF16)] + _cast_out_shapes(casts),
        grid=grid,
        in_specs=[
            pl.BlockSpec((tm, D_MODEL), lambda i, j: (i, 0)),
            pl.BlockSpec((D_MODEL, tn), lambda i, j: (0, j)),
            pl.BlockSpec((tm, RET_QK_DIM), lambda i, j: (i, 0)),
            pl.BlockSpec((tm, RET_QK_DIM), lambda i, j: (i, 0)),
            pl.BlockSpec((None, POOL_GROUPS, POOL_GROUP_DIM, POOL_GROUP_DIM),
                         lambda i, j: (layer, 0, 0, 0)),
            pl.BlockSpec((None, 1, POOL_WIDTH), lambda i, j: (layer, 0, 0)),
        ] + [c.in_spec for c in casts],
        out_specs=[pl.BlockSpec((tm, tn), lambda i, j: (i, j))] + _cast_out_specs(casts),
        scratch_shapes=[pltpu.VMEM((POOL_HALO, POOL_WIDTH), F32)],
        compiler_params=_params(2),
        name="in_proj",
    )(xn, w_in_b, cos, sin, pool_mix, pool_scale, *[c.stacked for c in casts])


def _retention_tables():
    c = RET_CHUNK
    log_gamma = jnp.log(1.0 - 2.0 ** (-5.0 - jnp.arange(RET_HEADS, dtype=F32)))
    idx = jnp.arange(c, dtype=F32)
    diff = idx[:, None] - idx[None, :]
    decay_mask = jnp.where(diff[None] >= 0,
                           jnp.exp(jnp.maximum(diff, 0.0)[None] * log_gamma[:, None, None]),
                           0.0)
    xi = jnp.exp((idx + 1.0)[None, :] * log_gamma[:, None])
    zeta = jnp.exp((c - 1.0 - idx)[None, :] * log_gamma[:, None])
    chunk_decay = jnp.exp(c * log_gamma)
    xi = jnp.broadcast_to(xi[:, :, None], (RET_HEADS, c, RET_QK_DIM))
    zeta = jnp.broadcast_to(zeta[:, :, None], (RET_HEADS, c, RET_QK_DIM))
    chunk_decay = jnp.broadcast_to(chunk_decay[:, None, None], (RET_HEADS, 1, RET_V_DIM))
    return decay_mask, xi, zeta, chunk_decay


def _retention_kernel(q_ref, k_ref, v_ref, g_ref, dmask_ref, xi_ref, zeta_ref, cdecay_ref,
                      o_ref, state_ref):
    c = RET_CHUNK

    @pl.when(pl.program_id(1) == 0)
    def _():
        state_ref[...] = jnp.zeros_like(state_ref)

    heads = range(RET_HEADS)
    qk_cols = [slice(h * RET_QK_DIM, (h + 1) * RET_QK_DIM) for h in heads]
    v_cols = [slice(h * RET_V_DIM, (h + 1) * RET_V_DIM) for h in heads]
    for n in range(q_ref.shape[0] // c):
        rows = slice(n * c, (n + 1) * c)
        q = [q_ref[rows, qk_cols[h]] for h in heads]
        k = [k_ref[rows, qk_cols[h]] for h in heads]
        v = [v_ref[rows, v_cols[h]] for h in heads]
        scores = [lax.dot_general(q[h], k[h], (((1,), (1,)), ((), ())),
                                  preferred_element_type=F32) for h in heads]
        kz = [(k[h].astype(F32) * zeta_ref[h]).astype(BF16) for h in heads]
        kv = [lax.dot_general(kz[h], v[h], (((0,), (0,)), ((), ())),
                              preferred_element_type=F32) for h in heads]
        state = [state_ref[h] for h in heads]
        lhs = [jnp.concatenate([(scores[h] * dmask_ref[h]).astype(BF16),
                                (q[h].astype(F32) * xi_ref[h]).astype(BF16)], axis=1)
               for h in heads]
        rhs = [jnp.concatenate([v[h], state[h].astype(BF16)], axis=0) for h in heads]
        y = [jnp.dot(lhs[h], rhs[h], preferred_element_type=F32) for h in heads]
        for h in heads:
            state_ref[h] = state[h] * cdecay_ref[h] + kv[h]
            yn = y[h] * lax.rsqrt(jnp.mean(y[h] * y[h], axis=-1, keepdims=True) + NORM_EPS)
            o_ref[rows, v_cols[h]] = (yn * g_ref[rows, v_cols[h]].astype(F32)).astype(o_ref.dtype)


def _retention(proj, tables, batch, seq):
    t = proj.shape[0]
    ts = 512
    spb = seq // ts
    dmask, xi, zeta, cdecay = tables
    tok_blk = lambda width, off: pl.BlockSpec((ts, width), lambda b, s: (b * spb + s, off // width))
    full_blk = lambda a: pl.BlockSpec(a.shape, lambda b, s: (0,) * a.ndim)
    return pl.pallas_call(
        _retention_kernel,
        out_shape=jax.ShapeDtypeStruct((t, RET_V_WIDTH), BF16),
        grid=(batch, spb),
        in_specs=[
            tok_blk(RET_QK_WIDTH, Q_OFF), tok_blk(RET_QK_WIDTH, K_OFF),
            tok_blk(RET_V_WIDTH, V_OFF), tok_blk(RET_V_WIDTH, G_OFF),
            full_blk(dmask), full_blk(xi), full_blk(zeta), full_blk(cdecay),
        ],
        out_specs=tok_blk(RET_V_WIDTH, 0),
        scratch_shapes=[pltpu.VMEM((RET_HEADS, RET_QK_DIM, RET_V_DIM), F32)],
        compiler_params=_params(2),
        name="retention",
    )(proj, proj, proj, proj, dmask, xi, zeta, cdecay)


def _merge_kernel(yr_ref, yp_ref, g0_ref, g1_ref, wr_ref, wp_ref, o_ref):
    yr = yr_ref[...]
    yp = yp_ref[...]
    for c in range(o_ref.shape[1] // MXU_COLS):
        cols = slice(c * MXU_COLS, (c + 1) * MXU_COLS)
        r = jnp.dot(yr, wr_ref[:, cols], preferred_element_type=F32)
        p = jnp.dot(yp, wp_ref[:, cols], preferred_element_type=F32)
        merged = g0_ref[:, cols].astype(F32) * r + g1_ref[:, cols].astype(F32) * p
        o_ref[:, cols] = merged.astype(o_ref.dtype)


def _merge(y_ret, proj, w_ret_up_b, w_pool_up_b):
    t = y_ret.shape[0]
    tm, tn = 1024, 1024
    assert POOL_WIDTH == tn
    return pl.pallas_call(
        _merge_kernel,
        out_shape=jax.ShapeDtypeStruct((t, D_MODEL), BF16),
        grid=(t // tm, D_MODEL // tn),
        in_specs=[
            pl.BlockSpec((tm, RET_V_WIDTH), lambda i, j: (i, 0)),
            pl.BlockSpec((tm, POOL_WIDTH), lambda i, j: (i, U_OFF // POOL_WIDTH)),
            pl.BlockSpec((tm, tn), lambda i, j: (i, GATE_OFF // tn + j)),
            pl.BlockSpec((tm, tn), lambda i, j: (i, (GATE_OFF + D_MODEL) // tn + j)),
            pl.BlockSpec((RET_V_WIDTH, tn), lambda i, j: (0, j)),
            pl.BlockSpec((POOL_WIDTH, tn), lambda i, j: (0, j)),
        ],
        out_specs=pl.BlockSpec((tm, tn), lambda i, j: (i, j)),
        compiler_params=_params(2),
        name="merge",
    )(y_ret, proj, proj, proj, w_ret_up_b, w_pool_up_b)


def _residual_norm_kernel(*refs, emit_residual):
    x_ref, a_ref, w_ref, gain_ref = refs[:4]
    if emit_residual:
        res_ref, xn_ref = refs[4:]
    else:
        (xn_ref,) = refs[4:]
        res_ref = xn_ref
    a = a_ref[...]
    for c in range(res_ref.shape[1] // MXU_COLS):
        cols = slice(c * MXU_COLS, (c + 1) * MXU_COLS)
        res_ref[:, cols] = x_ref[:, cols] + jnp.dot(a, w_ref[:, cols], preferred_element_type=F32)
    _rmsnorm_rows(res_ref, gain_ref, xn_ref)


def _residual_norm(x, a, w_b, gain, tm, name, xn_dtype, emit_residual=True):
    t, k = a.shape
    row_blk = lambda: pl.BlockSpec((tm, D_MODEL), lambda i: (i, 0))
    out_shape = [jax.ShapeDtypeStruct(x.shape, xn_dtype)]
    out_specs = [row_blk()]
    if emit_residual:
        out_shape.insert(0, jax.ShapeDtypeStruct(x.shape, x.dtype))
        out_specs.insert(0, row_blk())
    return pl.pallas_call(
        functools.partial(_residual_norm_kernel, emit_residual=emit_residual),
        out_shape=out_shape,
        grid=(t // tm,),
        in_specs=[
            row_blk(),
            pl.BlockSpec((tm, k), lambda i: (i, 0)),
            pl.BlockSpec((k, D_MODEL), lambda i: (0, 0), pipeline_mode=pl.Buffered(1)),
            pl.BlockSpec((1, D_MODEL), lambda i: (0, 0)),
        ],
        out_specs=out_specs,
        compiler_params=_params(1),
        name=name,
    )(x, a, w_b, gain)


def _ffn_up_kernel(*refs, d_ff, n_cast_src, n_cast_dst):
    xn_ref, wg_ref, wu_ref = refs[:3]
    cast_src = refs[3:3 + n_cast_src]
    o_ref = refs[3 + n_cast_src]
    cast_dst = refs[4 + n_cast_src:4 + n_cast_src + n_cast_dst]
    j = pl.program_id(1)
    tn = o_ref.shape[1]

    def body(n_blocks):
        _run_casts(cast_src, cast_dst)
        xn = xn_ref[...]
        for c in range(n_blocks):
            cols = slice(c * MXU_COLS, (c + 1) * MXU_COLS)
            gate = jnp.dot(xn, wg_ref[:, cols], preferred_element_type=F32)
            up = jnp.dot(xn, wu_ref[:, cols], preferred_element_type=F32)
            o_ref[:, cols] = (_silu(gate) * up).astype(o_ref.dtype)

    full_steps, tail = divmod(d_ff, tn)

    @pl.when(j < full_steps)
    def _():
        body(tn // MXU_COLS)

    if tail:
        @pl.when(j == full_steps)
        def _():
            body(tail // MXU_COLS)


def _ffn_up(xn, w_g_b, w_u_b, casts_of):
    t = xn.shape[0]
    d_ff = w_g_b.shape[1]
    tm, tn = 1024, 1024
    assert d_ff % MXU_COLS == 0
    grid = (t // tm, pl.cdiv(d_ff, tn))
    casts = casts_of(grid)
    kern = functools.partial(_ffn_up_kernel, d_ff=d_ff, n_cast_src=len(casts),
                             n_cast_dst=len(_cast_out_specs(casts)))
    return pl.pallas_call(
        kern,
        out_shape=[jax.ShapeDtypeStruct((t, d_ff), BF16)] + _cast_out_shapes(casts),
        grid=grid,
        in_specs=[
            pl.BlockSpec((tm, D_MODEL), lambda i, j: (i, 0)),
            pl.BlockSpec((D_MODEL, tn), lambda i, j: (0, j)),
            pl.BlockSpec((D_MODEL, tn), lambda i, j: (0, j)),
        ] + [c.in_spec for c in casts],
        out_specs=[pl.BlockSpec((tm, tn), lambda i, j: (i, j))] + _cast_out_specs(casts),
        compiler_params=_params(2),
        name="ffn_up",
    )(xn, w_g_b, w_u_b, *[c.stacked for c in casts])


def kernel(x, positions, norm1, w_in, pool_mix, pool_scale, w_ret_up, w_pool_up, w_o,
           norm2, w_gu, w_down, norm_f):
    batch, seq, _ = x.shape
    depth = w_in.shape[0]
    t = batch * seq
    xt = x.reshape(t, D_MODEL)

    pool_mix_b = pool_mix.astype(BF16)
    pool_scale_r = pool_scale.reshape(depth, 1, POOL_WIDTH)

    cos, sin = _rope_tables(positions)
    tables = _retention_tables()

    w_in_b, w_ret_up_b, w_pool_up_b, w_o_b = (
        w[0].astype(BF16) for w in (w_in, w_ret_up, w_pool_up, w_o))
    xn = _norm(xt, norm1[0][None, :], BF16)
    for layer in range(depth):
        last = layer + 1 == depth
        proj, w_g_b, w_u_b, w_down_b = _inproj(
            xn, w_in_b, cos, sin, pool_mix_b, pool_scale_r, layer, seq,
            lambda grid: [_cast_stream(w_gu, layer, grid, n_split=2),
                          _cast_stream(w_down, layer, grid)])
        y_ret = _retention(proj, tables, batch, seq)
        merged = _merge(y_ret, proj, w_ret_up_b, w_pool_up_b)
        xt, xn = _residual_norm(xt, merged, w_o_b, norm2[layer][None, :], 512, "out_proj", BF16)
        hidden, *next_b = _ffn_up(
            xn, w_g_b, w_u_b,
            lambda grid: [] if last else [
                _cast_stream(w, layer + 1, grid) for w in (w_in, w_ret_up, w_pool_up, w_o)])
        if last:
            (out,) = _residual_norm(xt, hidden, w_down_b, norm_f[None, :], 512, "ffn_down", F32,
                                    emit_residual=False)
        else:
            w_in_b, w_ret_up_b, w_pool_up_b, w_o_b = next_b
            xt, xn = _residual_norm(xt, hidden, w_down_b, norm1[layer + 1][None, :], 512,
                                    "ffn_down", BF16)
    return out.reshape(batch, seq, D_MODEL)
```

```python
import functools
from typing import NamedTuple

import jax
import jax.numpy as jnp
from jax import lax
from jax.experimental import pallas as pl
from jax.experimental.pallas import tpu as pltpu

D_MODEL = 2048
RET_HEADS = 8
RET_QK_DIM = 128
RET_V_DIM = 256
RET_QK_WIDTH = RET_HEADS * RET_QK_DIM
RET_V_WIDTH = RET_HEADS * RET_V_DIM
RET_CHUNK = 128
ROPE_BASE = 10000.0
POOL_WINDOWS = (2, 4, 8, 16)
POOL_GROUPS = len(POOL_WINDOWS)
POOL_GROUP_DIM = D_MODEL // 8
POOL_WIDTH = POOL_GROUPS * POOL_GROUP_DIM
N_BRANCHES = 2
IN_WIDTH = 2 * RET_QK_WIDTH + 2 * RET_V_WIDTH + POOL_WIDTH + N_BRANCHES * D_MODEL
NORM_EPS = 1e-6

Q_OFF = 0
K_OFF = RET_QK_WIDTH
V_OFF = 2 * RET_QK_WIDTH
G_OFF = V_OFF + RET_V_WIDTH
U_OFF = G_OFF + RET_V_WIDTH
GATE_OFF = U_OFF + POOL_WIDTH

POOL_HALO = 16
assert POOL_HALO >= max(POOL_WINDOWS) and POOL_HALO % 8 == 0

VMEM_LIMIT_BYTES = 56 * 1024 * 1024
MXU_COLS = 256
assert POOL_GROUP_DIM == MXU_COLS

BF16 = jnp.bfloat16
F32 = jnp.float32


def _params(n_axes):
    return pltpu.CompilerParams(
        dimension_semantics=("arbitrary",) * n_axes,
        vmem_limit_bytes=VMEM_LIMIT_BYTES,
    )


def _rope_kernel(pos_ref, freq_ref, cos_ref, sin_ref):
    ang = pos_ref[...] * freq_ref[...]
    lane = lax.broadcasted_iota(jnp.int32, ang.shape, 1)
    sin = jnp.sin(ang)
    cos_ref[...] = jnp.cos(ang)
    sin_ref[...] = jnp.where(lane < RET_QK_DIM // 2, -sin, sin)


def _rope_tables(positions):
    t = positions.size
    half = RET_QK_DIM // 2
    inv_freq = ROPE_BASE ** (-jnp.arange(half, dtype=F32) / half)
    freq = jnp.concatenate([inv_freq, inv_freq])[None, :]
    pos = jnp.broadcast_to(positions.reshape(t, 1).astype(F32), (t, RET_QK_DIM))
    tm = 1024
    spec = pl.BlockSpec((tm, RET_QK_DIM), lambda i: (i, 0))
    return pl.pallas_call(
        _rope_kernel,
        out_shape=(jax.ShapeDtypeStruct((t, RET_QK_DIM), F32),) * 2,
        grid=(t // tm,),
        in_specs=[spec, pl.BlockSpec((1, RET_QK_DIM), lambda i: (0, 0))],
        out_specs=(spec, spec),
        compiler_params=_params(1),
        name="rope_tables",
    )(pos, freq)


BF16_SUBLANES = 16


class _CastStream(NamedTuple):
    stacked: jax.Array
    in_spec: pl.BlockSpec
    out_specs: tuple
    out_shapes: tuple


def _cast_stream(stacked, layer, grid, n_split=1):
    _, k, n = stacked.shape
    n_steps = grid[0] * grid[1]
    rows = BF16_SUBLANES
    while k % rows or k // rows > n_steps:
        rows += BF16_SUBLANES
    last = k // rows - 1
    block = lambda i, j: jnp.minimum(i * grid[1] + j, last)
    return _CastStream(
        stacked,
        pl.BlockSpec((None, rows, n), lambda i, j: (layer, block(i, j), 0)),
        (pl.BlockSpec((rows, n // n_split), lambda i, j: (block(i, j), 0)),) * n_split,
        (jax.ShapeDtypeStruct((k, n // n_split), BF16),) * n_split,
    )


def _cast_out_specs(casts):
    return [s for c in casts for s in c.out_specs]


def _cast_out_shapes(casts):
    return [s for c in casts for s in c.out_shapes]


def _run_casts(src_refs, dst_refs):
    d = 0
    for src in src_refs:
        col = 0
        while col < src.shape[1]:
            dst = dst_refs[d]
            dst[...] = src[:, col:col + dst.shape[1]].astype(dst.dtype)
            col += dst.shape[1]
            d += 1
    assert d == len(dst_refs)


def _sigmoid(x):
    return 0.5 * jnp.tanh(0.5 * x) + 0.5


def _silu(x):
    h = 0.5 * x
    return h + h * jnp.tanh(h)


def _rotary(t, cos, sin):
    outs = []
    for h in range(t.shape[1] // RET_QK_DIM):
        th = t[:, h * RET_QK_DIM:(h + 1) * RET_QK_DIM]
        outs.append(th * cos + pltpu.roll(th, RET_QK_DIM // 2, axis=1) * sin)
    return jnp.concatenate(outs, axis=1)


def _rmsnorm_rows(x_ref, gain_ref, out_ref):
    n_blocks = x_ref.shape[1] // MXU_COLS
    col = lambda c: slice(c * MXU_COLS, (c + 1) * MXU_COLS)
    sq = None
    for c in range(n_blocks):
        xc = x_ref[:, col(c)]
        sq = xc * xc if sq is None else sq + xc * xc
    scale = lax.rsqrt(jnp.sum(sq, axis=-1, keepdims=True) / x_ref.shape[1] + NORM_EPS)
    for c in range(n_blocks):
        out_ref[:, col(c)] = (x_ref[:, col(c)] * scale * gain_ref[:, col(c)]).astype(out_ref.dtype)


def _norm_kernel(x_ref, g_ref, o_ref):
    _rmsnorm_rows(x_ref, g_ref, o_ref)


def _norm(x, gain, out_dtype):
    t = x.shape[0]
    tm = 512
    return pl.pallas_call(
        _norm_kernel,
        out_shape=jax.ShapeDtypeStruct(x.shape, out_dtype),
        grid=(t // tm,),
        in_specs=[
            pl.BlockSpec((tm, D_MODEL), lambda i: (i, 0)),
            pl.BlockSpec((1, D_MODEL), lambda i: (0, 0)),
        ],
        out_specs=pl.BlockSpec((tm, D_MODEL), lambda i: (i, 0)),
        compiler_params=_params(1),
        name="input_norm",
    )(x, gain)


SECTION = 1024
SECTIONS_PER_STEP = 1
N_SECTIONS = IN_WIDTH // SECTION
assert all(off % SECTION == 0 for off in (K_OFF, V_OFF, G_OFF, U_OFF, GATE_OFF))
assert RET_QK_WIDTH == SECTION and POOL_WIDTH == SECTION


def _inproj_kernel(*refs, tiles_per_seq, n_cast_src, n_cast_dst):
    xn_ref, w_ref, cos_ref, sin_ref, pmix_ref, pscale_ref = refs[:6]
    cast_src = refs[6:6 + n_cast_src]
    o_ref = refs[6 + n_cast_src]
    cast_dst = refs[7 + n_cast_src:7 + n_cast_src + n_cast_dst]
    halo_ref = refs[7 + n_cast_src + n_cast_dst]
    i = pl.program_id(0)
    j = pl.program_id(1)
    tm = xn_ref.shape[0]

    def project(cols):
        return jnp.dot(xn_ref[...], w_ref[:, cols], preferred_element_type=F32)

    def rotary_section(cols, section):
        scale = jnp.where(section == Q_OFF // SECTION, RET_QK_DIM ** -0.5, 1.0).astype(F32)
        rot = _rotary(project(cols), cos_ref[...] * scale, sin_ref[...] * scale)
        o_ref[:, cols] = rot.astype(o_ref.dtype)

    def plain_section(cols):
        o_ref[:, cols] = project(cols).astype(o_ref.dtype)

    def silu_section(cols):
        o_ref[:, cols] = _silu(project(cols)).astype(o_ref.dtype)

    def sigmoid_section(cols):
        o_ref[:, cols] = _sigmoid(project(cols)).astype(o_ref.dtype)

    def pool_section(cols):
        seq_tile = i % tiles_per_seq

        @pl.when(seq_tile == 0)
        def _():
            halo_ref[...] = jnp.zeros_like(halo_ref)

        row = lax.broadcasted_iota(jnp.int32, (tm, 1), 0) + seq_tile * tm
        pos1 = (row + 1).astype(F32)
        xn = xn_ref[...]
        group = lambda gi: slice(gi * POOL_GROUP_DIM, (gi + 1) * POOL_GROUP_DIM)
        shifted = lambda gi: slice(cols.start + gi * POOL_GROUP_DIM,
                                   cols.start + (gi + 1) * POOL_GROUP_DIM)

        def pool_and_mix(gi, u):
            s = jnp.concatenate([halo_ref[:, group(gi)], u], axis=0)
            halo_ref[:, group(gi)] = u[tm - POOL_HALO:, :]
            shift = 1
            while shift < POOL_WINDOWS[gi]:
                s = s + pltpu.roll(s, shift, axis=0)
                shift *= 2
            count = jnp.minimum(pos1, float(POOL_WINDOWS[gi]))
            pooled = s[POOL_HALO:, :] / count - u
            mixed = jnp.dot(pooled.astype(BF16), pmix_ref[gi], preferred_element_type=F32)
            o_ref[:, shifted(gi)] = (mixed * pscale_ref[:, group(gi)]).astype(o_ref.dtype)

        u_prev = None
        for gi in range(POOL_GROUPS):
            u = jnp.dot(xn, w_ref[:, shifted(gi)], preferred_element_type=F32)
            if u_prev is not None:
                pool_and_mix(gi - 1, u_prev)
            u_prev = u
        pool_and_mix(POOL_GROUPS - 1, u_prev)

    def section_fn(index):
        off = index * SECTION
        if off < V_OFF:
            return rotary_section
        if off < G_OFF:
            return plain_section
        if off < U_OFF:
            return silu_section
        if off == U_OFF:
            return pool_section
        return sigmoid_section

    n_steps = pl.cdiv(N_SECTIONS, SECTIONS_PER_STEP)
    step_fns = [tuple(section_fn(s * SECTIONS_PER_STEP + h) for h in range(SECTIONS_PER_STEP)
                      if s * SECTIONS_PER_STEP + h < N_SECTIONS) for s in range(n_steps)]
    lo = 0
    while lo < n_steps:
        hi = lo + 1
        while hi < n_steps and step_fns[hi] == step_fns[lo]:
            hi += 1

        @pl.when((j >= lo) & (j < hi))
        def _(fns=step_fns[lo]):
            _run_casts(cast_src, cast_dst)
            for half, fn in enumerate(fns):
                cols = slice(half * SECTION, (half + 1) * SECTION)
                if fn is rotary_section:
                    fn(cols, j * SECTIONS_PER_STEP + half)
                else:
                    fn(cols)

        lo = hi


def _inproj(xn, w_in_b, cos, sin, pool_mix, pool_scale, layer, seq, casts_of):
    t = xn.shape[0]
    tm, tn = 1024, SECTION * SECTIONS_PER_STEP
    assert seq % tm == 0
    grid = (t // tm, pl.cdiv(IN_WIDTH, tn))
    casts = casts_of(grid)
    kern = functools.partial(_inproj_kernel, tiles_per_seq=seq // tm, n_cast_src=len(casts),
                             n_cast_dst=len(_cast_out_specs(casts)))
    return pl.pallas_call(
        kern,
        out_shape=[jax.ShapeDtypeStruct((t, IN_WIDTH), BF16)] + _cast_out_shapes(casts),
        grid=grid,
        in_specs=[
            pl.BlockSpec((tm, D_MODEL), lambda i, j: (i, 0)),
            pl.BlockSpec((D_MODEL, tn), lambda i, j: (0, j)),
            pl.BlockSpec((tm, RET_QK_DIM), lambda i, j: (i, 0)),
            pl.BlockSpec((tm, RET_QK_DIM), lambda i, j: (i, 0)),
            pl.BlockSpec((None, POOL_GROUPS, POOL_GROUP_DIM, POOL_GROUP_DIM),
                         lambda i, j: (layer, 0, 0, 0)),
            pl.BlockSpec((None, 1, POOL_WIDTH), lambda i, j: (layer, 0, 0)),
        ] + [c.in_spec for c in casts],
        out_specs=[pl.BlockSpec((tm, tn), lambda i, j: (i, j))] + _cast_out_specs(casts),
        scratch_shapes=[pltpu.VMEM((POOL_HALO, POOL_WIDTH), F32)],
        compiler_params=_params(2),
        name="in_proj",
    )(xn, w_in_b, cos, sin, pool_mix, pool_scale, *[c.stacked for c in casts])


def _retention_tables():
    c = RET_CHUNK
    log_gamma = jnp.log(1.0 - 2.0 ** (-5.0 - jnp.arange(RET_HEADS, dtype=F32)))
    idx = jnp.arange(c, dtype=F32)
    diff = idx[:, None] - idx[None, :]
    decay_mask = jnp.where(diff[None] >= 0,
                           jnp.exp(jnp.maximum(diff, 0.0)[None] * log_gamma[:, None, None]),
                           0.0)
    xi = jnp.exp((idx + 1.0)[None, :] * log_gamma[:, None])
    zeta = jnp.exp((c - 1.0 - idx)[None, :] * log_gamma[:, None])
    chunk_decay = jnp.exp(c * log_gamma)
    xi = jnp.broadcast_to(xi[:, :, None], (RET_HEADS, c, RET_QK_DIM))
    zeta = jnp.broadcast_to(zeta[:, :, None], (RET_HEADS, c, RET_QK_DIM))
    chunk_decay = jnp.broadcast_to(chunk_decay[:, None, None], (RET_HEADS, 1, RET_V_DIM))
    return decay_mask, xi, zeta, chunk_decay


def _retention_kernel(q_ref, k_ref, v_ref, g_ref, dmask_ref, xi_ref, zeta_ref, cdecay_ref,
                      o_ref, state_ref):
    c = RET_CHUNK

    @pl.when(pl.program_id(1) == 0)
    def _():
        state_ref[...] = jnp.zeros_like(state_ref)

    heads = range(RET_HEADS)
    qk_cols = [slice(h * RET_QK_DIM, (h + 1) * RET_QK_DIM) for h in heads]
    v_cols = [slice(h * RET_V_DIM, (h + 1) * RET_V_DIM) for h in heads]
    for n in range(q_ref.shape[0] // c):
        rows = slice(n * c, (n + 1) * c)
        q = [q_ref[rows, qk_cols[h]] for h in heads]
        k = [k_ref[rows, qk_cols[h]] for h in heads]
        v = [v_ref[rows, v_cols[h]] for h in heads]
        scores = [lax.dot_general(q[h], k[h], (((1,), (1,)), ((), ())),
                                  preferred_element_type=F32) for h in heads]
        kz = [(k[h].astype(F32) * zeta_ref[h]).astype(BF16) for h in heads]
        kv = [lax.dot_general(kz[h], v[h], (((0,), (0,)), ((), ())),
                              preferred_element_type=F32) for h in heads]
        state = [state_ref[h] for h in heads]
        lhs = [jnp.concatenate([(scores[h] * dmask_ref[h]).astype(BF16),
                                (q[h].astype(F32) * xi_ref[h]).astype(BF16)], axis=1)
               for h in heads]
        rhs = [jnp.concatenate([v[h], state[h].astype(BF16)], axis=0) for h in heads]
        y = [jnp.dot(lhs[h], rhs[h], preferred_element_type=F32) for h in heads]
        for h in heads:
            state_ref[h] = state[h] * cdecay_ref[h] + kv[h]
            yn = y[h] * lax.rsqrt(jnp.mean(y[h] * y[h], axis=-1, keepdims=True) + NORM_EPS)
            o_ref[rows, v_cols[h]] = (yn * g_ref[rows, v_cols[h]].astype(F32)).astype(o_ref.dtype)


def _retention(proj, tables, batch, seq):
    t = proj.shape[0]
    ts = 512
    spb = seq // ts
    dmask, xi, zeta, cdecay = tables
    tok_blk = lambda width, off: pl.BlockSpec((ts, width), lambda b, s: (b * spb + s, off // width))
    full_blk = lambda a: pl.BlockSpec(a.shape, lambda b, s: (0,) * a.ndim)
    return pl.pallas_call(
        _retention_kernel,
        out_shape=jax.ShapeDtypeStruct((t, RET_V_WIDTH), BF16),
        grid=(batch, spb),
        in_specs=[
            tok_blk(RET_QK_WIDTH, Q_OFF), tok_blk(RET_QK_WIDTH, K_OFF),
            tok_blk(RET_V_WIDTH, V_OFF), tok_blk(RET_V_WIDTH, G_OFF),
            full_blk(dmask), full_blk(xi), full_blk(zeta), full_blk(cdecay),
        ],
        out_specs=tok_blk(RET_V_WIDTH, 0),
        scratch_shapes=[pltpu.VMEM((RET_HEADS, RET_QK_DIM, RET_V_DIM), F32)],
        compiler_params=_params(2),
        name="retention",
    )(proj, proj, proj, proj, dmask, xi, zeta, cdecay)


def _merge_kernel(yr_ref, yp_ref, g0_ref, g1_ref, wr_ref, wp_ref, o_ref):
    yr = yr_ref[...]
    yp = yp_ref[...]
    for c in range(o_ref.shape[1] // MXU_COLS):
        cols = slice(c * MXU_COLS, (c + 1) * MXU_COLS)
        r = jnp.dot(yr, wr_ref[:, cols], preferred_element_type=F32)
        p = jnp.dot(yp, wp_ref[:, cols], preferred_element_type=F32)
        merged = g0_ref[:, cols].astype(F32) * r + g1_ref[:, cols].astype(F32) * p
        o_ref[:, cols] = merged.astype(o_ref.dtype)


def _merge(y_ret, proj, w_ret_up_b, w_pool_up_b):
    t = y_ret.shape[0]
    tm, tn = 1024, 1024
    assert POOL_WIDTH == tn
    return pl.pallas_call(
        _merge_kernel,
        out_shape=jax.ShapeDtypeStruct((t, D_MODEL), BF16),
        grid=(t // tm, D_MODEL // tn),
        in_specs=[
            pl.BlockSpec((tm, RET_V_WIDTH), lambda i, j: (i, 0)),
            pl.BlockSpec((tm, POOL_WIDTH), lambda i, j: (i, U_OFF // POOL_WIDTH)),
            pl.BlockSpec((tm, tn), lambda i, j: (i, GATE_OFF // tn + j)),
            pl.BlockSpec((tm, tn), lambda i, j: (i, (GATE_OFF + D_MODEL) // tn + j)),
            pl.BlockSpec((RET_V_WIDTH, tn), lambda i, j: (0, j)),
            pl.BlockSpec((POOL_WIDTH, tn), lambda i, j: (0, j)),
        ],
        out_specs=pl.BlockSpec((tm, tn), lambda i, j: (i, j)),
        compiler_params=_params(2),
        name="merge",
    )(y_ret, proj, proj, proj, w_ret_up_b, w_pool_up_b)


def _residual_norm_kernel(*refs, emit_residual):
    x_ref, a_ref, w_ref, gain_ref = refs[:4]
    if emit_residual:
        res_ref, xn_ref = refs[4:]
    else:
        (xn_ref,) = refs[4:]
        res_ref = xn_ref
    a = a_ref[...]
    for c in range(res_ref.shape[1] // MXU_COLS):
        cols = slice(c * MXU_COLS, (c + 1) * MXU_COLS)
        res_ref[:, cols] = x_ref[:, cols] + jnp.dot(a, w_ref[:, cols], preferred_element_type=F32)
    _rmsnorm_rows(res_ref, gain_ref, xn_ref)


def _residual_norm(x, a, w_b, gain, tm, name, xn_dtype, emit_residual=True):
    t, k = a.shape
    row_blk = lambda: pl.BlockSpec((tm, D_MODEL), lambda i: (i, 0))
    out_shape = [jax.ShapeDtypeStruct(x.shape, xn_dtype)]
    out_specs = [row_blk()]
    if emit_residual:
        out_shape.insert(0, jax.ShapeDtypeStruct(x.shape, x.dtype))
        out_specs.insert(0, row_blk())
    return pl.pallas_call(
        functools.partial(_residual_norm_kernel, emit_residual=emit_residual),
        out_shape=out_shape,
        grid=(t // tm,),
        in_specs=[
            row_blk(),
            pl.BlockSpec((tm, k), lambda i: (i, 0)),
            pl.BlockSpec((k, D_MODEL), lambda i: (0, 0), pipeline_mode=pl.Buffered(1)),
            pl.BlockSpec((1, D_MODEL), lambda i: (0, 0)),
        ],
        out_specs=out_specs,
        compiler_params=_params(1),
        name=name,
    )(x, a, w_b, gain)


def _ffn_up_kernel(*refs, d_ff, n_cast_src, n_cast_dst):
    xn_ref, wg_ref, wu_ref = refs[:3]
    cast_src = refs[3:3 + n_cast_src]
    o_ref = refs[3 + n_cast_src]
    cast_dst = refs[4 + n_cast_src:4 + n_cast_src + n_cast_dst]
    j = pl.program_id(1)
    tn = o_ref.shape[1]

    def body(n_blocks):
        _run_casts(cast_src, cast_dst)
        xn = xn_ref[...]
        for c in range(n_blocks):
            cols = slice(c * MXU_COLS, (c + 1) * MXU_COLS)
            gate = jnp.dot(xn, wg_ref[:, cols], preferred_element_type=F32)
            up = jnp.dot(xn, wu_ref[:, cols], preferred_element_type=F32)
            o_ref[:, cols] = (_silu(gate) * up).astype(o_ref.dtype)

    full_steps, tail = divmod(d_ff, tn)

    @pl.when(j < full_steps)
    def _():
        body(tn // MXU_COLS)

    if tail:
        @pl.when(j == full_steps)
        def _():
            body(tail // MXU_COLS)


def _ffn_up(xn, w_g_b, w_u_b, casts_of):
    t = xn.shape[0]
    d_ff = w_g_b.shape[1]
    tm, tn = 1024, 1024
    assert d_ff % MXU_COLS == 0
    grid = (t // tm, pl.cdiv(d_ff, tn))
    casts = casts_of(grid)
    kern = functools.partial(_ffn_up_kernel, d_ff=d_ff, n_cast_src=len(casts),
                             n_cast_dst=len(_cast_out_specs(casts)))
    return pl.pallas_call(
        kern,
        out_shape=[jax.ShapeDtypeStruct((t, d_ff), BF16)] + _cast_out_shapes(casts),
        grid=grid,
        in_specs=[
            pl.BlockSpec((tm, D_MODEL), lambda i, j: (i, 0)),
            pl.BlockSpec((D_MODEL, tn), lambda i, j: (0, j)),
            pl.BlockSpec((D_MODEL, tn), lambda i, j: (0, j)),
        ] + [c.in_spec for c in casts],
        out_specs=[pl.BlockSpec((tm, tn), lambda i, j: (i, j))] + _cast_out_specs(casts),
        compiler_params=_params(2),
        name="ffn_up",
    )(xn, w_g_b, w_u_b, *[c.stacked for c in casts])


def kernel(x, positions, norm1, w_in, pool_mix, pool_scale, w_ret_up, w_pool_up, w_o,
           norm2, w_gu, w_down, norm_f):
    batch, seq, _ = x.shape
    depth = w_in.shape[0]
    t = batch * seq
    xt = x.reshape(t, D_MODEL)

    pool_mix_b = pool_mix.astype(BF16)
    pool_scale_r = pool_scale.reshape(depth, 1, POOL_WIDTH)

    cos, sin = _rope_tables(positions)
    tables = _retention_tables()

    w_in_b, w_ret_up_b, w_pool_up_b, w_o_b = (
        w[0].astype(BF16) for w in (w_in, w_ret_up, w_pool_up, w_o))
    xn = _norm(xt, norm1[0][None, :], BF16)
    for layer in range(depth):
        last = layer + 1 == depth
        proj, w_g_b, w_u_b, w_down_b = _inproj(
            xn, w_in_b, cos, sin, pool_mix_b, pool_scale_r, layer, seq,
            lambda grid: [_cast_stream(w_gu, layer, grid, n_split=2),
                          _cast_stream(w_down, layer, grid)])
        y_ret = _retention(proj, tables, batch, seq)
        merged = _merge(y_ret, proj, w_ret_up_b, w_pool_up_b)
        xt, xn = _residual_norm(xt, merged, w_o_b, norm2[layer][None, :], 512, "out_proj", BF16)
        hidden, *next_b = _ffn_up(
            xn, w_g_b, w_u_b,
            lambda grid: [] if last else [
                _cast_stream(w, layer + 1, grid) for w in (w_in, w_ret_up, w_pool_up, w_o)])
        if last:
            (out,) = _residual_norm(xt, hidden, w_down_b, norm_f[None, :], 512, "ffn_down", F32,
                                    emit_residual=False)
        else:
            w_in_b, w_ret_up_b, w_pool_up_b, w_o_b = next_b
            xt, xn = _residual_norm(xt, hidden, w_down_b, norm1[layer + 1][None, :], 512,
                                    "ffn_down", BF16)
    return out.reshape(batch, seq, D_MODEL)
```

```python
import functools
from typing import NamedTuple

import jax
import jax.numpy as jnp
from jax import lax
from jax.experimental import pallas as pl
from jax.experimental.pallas import tpu as pltpu

D_MODEL = 2048
RET_HEADS = 8
RET_QK_DIM = 128
RET_V_DIM = 256
RET_QK_WIDTH = RET_HEADS * RET_QK_DIM
RET_V_WIDTH = RET_HEADS * RET_V_DIM
RET_CHUNK = 128
ROPE_BASE = 10000.0
POOL_WINDOWS = (2, 4, 8, 16)
POOL_GROUPS = len(POOL_WINDOWS)
POOL_GROUP_DIM = D_MODEL // 8
POOL_WIDTH = POOL_GROUPS * POOL_GROUP_DIM
N_BRANCHES = 2
IN_WIDTH = 2 * RET_QK_WIDTH + 2 * RET_V_WIDTH + POOL_WIDTH + N_BRANCHES * D_MODEL
NORM_EPS = 1e-6

Q_OFF = 0
K_OFF = RET_QK_WIDTH
V_OFF = 2 * RET_QK_WIDTH
G_OFF = V_OFF + RET_V_WIDTH
U_OFF = G_OFF + RET_V_WIDTH
GATE_OFF = U_OFF + POOL_WIDTH

POOL_HALO = 16
assert POOL_HALO >= max(POOL_WINDOWS) and POOL_HALO % 8 == 0

VMEM_LIMIT_BYTES = 56 * 1024 * 1024
MXU_COLS = 256
assert POOL_GROUP_DIM == MXU_COLS

BF16 = jnp.bfloat16
F32 = jnp.float32


def _params(n_axes):
    return pltpu.CompilerParams(
        dimension_semantics=("arbitrary",) * n_axes,
        vmem_limit_bytes=VMEM_LIMIT_BYTES,
    )


def _rope_kernel(pos_ref, freq_ref, cos_ref, sin_ref):
    ang = pos_ref[...] * freq_ref[...]
    lane = lax.broadcasted_iota(jnp.int32, ang.shape, 1)
    sin = jnp.sin(ang)
    cos_ref[...] = jnp.cos(ang)
    sin_ref[...] = jnp.where(lane < RET_QK_DIM // 2, -sin, sin)


def _rope_tables(positions):
    t = positions.size
    half = RET_QK_DIM // 2
    inv_freq = ROPE_BASE ** (-jnp.arange(half, dtype=F32) / half)
    freq = jnp.concatenate([inv_freq, inv_freq])[None, :]
    pos = jnp.broadcast_to(positions.reshape(t, 1).astype(F32), (t, RET_QK_DIM))
    tm = 1024
    spec = pl.BlockSpec((tm, RET_QK_DIM), lambda i: (i, 0))
    return pl.pallas_call(
        _rope_kernel,
        out_shape=(jax.ShapeDtypeStruct((t, RET_QK_DIM), F32),) * 2,
        grid=(t // tm,),
        in_specs=[spec, pl.BlockSpec((1, RET_QK_DIM), lambda i: (0, 0))],
        out_specs=(spec, spec),
        compiler_params=_params(1),
        name="rope_tables",
    )(pos, freq)


BF16_SUBLANES = 16


class _CastStream(NamedTuple):
    stacked: jax.Array
    in_spec: pl.BlockSpec
    out_specs: tuple
    out_shapes: tuple


def _cast_stream(stacked, layer, grid, n_split=1):
    _, k, n = stacked.shape
    n_steps = grid[0] * grid[1]
    rows = BF16_SUBLANES
    while k % rows or k // rows > n_steps:
        rows += BF16_SUBLANES
    last = k // rows - 1
    block = lambda i, j: jnp.minimum(i * grid[1] + j, last)
    return _CastStream(
        stacked,
        pl.BlockSpec((None, rows, n), lambda i, j: (layer, block(i, j), 0)),
        (pl.BlockSpec((rows, n // n_split), lambda i, j: (block(i, j), 0)),) * n_split,
        (jax.ShapeDtypeStruct((k, n // n_split), BF16),) * n_split,
    )


def _cast_out_specs(casts):
    return [s for c in casts for s in c.out_specs]


def _cast_out_shapes(casts):
    return [s for c in casts for s in c.out_shapes]


def _run_casts(src_refs, dst_refs, part=0, n_parts=1):
    d = 0
    for src in src_refs:
        col = 0
        while col < src.shape[1]:
            dst = dst_refs[d]
            width = dst.shape[1] // n_parts
            lo = part * width
            dst[:, lo:lo + width] = src[:, col + lo:col + lo + width].astype(dst.dtype)
            col += dst.shape[1]
            d += 1
    assert d == len(dst_refs)


def _sigmoid(x):
    return 0.5 * jnp.tanh(0.5 * x) + 0.5


def _silu(x):
    h = 0.5 * x
    return h + h * jnp.tanh(h)


def _rotary(t, cos, sin):
    outs = []
    for h in range(t.shape[1] // RET_QK_DIM):
        th = t[:, h * RET_QK_DIM:(h + 1) * RET_QK_DIM]
        outs.append(th * cos + pltpu.roll(th, RET_QK_DIM // 2, axis=1) * sin)
    return jnp.concatenate(outs, axis=1)


def _rmsnorm_rows(x_ref, gain_ref, out_ref):
    n_blocks = x_ref.shape[1] // MXU_COLS
    col = lambda c: slice(c * MXU_COLS, (c + 1) * MXU_COLS)
    sq = None
    for c in range(n_blocks):
        xc = x_ref[:, col(c)]
        sq = xc * xc if sq is None else sq + xc * xc
    scale = lax.rsqrt(jnp.sum(sq, axis=-1, keepdims=True) / x_ref.shape[1] + NORM_EPS)
    for c in range(n_blocks):
        out_ref[:, col(c)] = (x_ref[:, col(c)] * scale * gain_ref[:, col(c)]).astype(out_ref.dtype)


def _norm_kernel(x_ref, g_ref, o_ref):
    _rmsnorm_rows(x_ref, g_ref, o_ref)


def _norm(x, gain, out_dtype):
    t = x.shape[0]
    tm = 512
    return pl.pallas_call(
        _norm_kernel,
        out_shape=jax.ShapeDtypeStruct(x.shape, out_dtype),
        grid=(t // tm,),
        in_specs=[
            pl.BlockSpec((tm, D_MODEL), lambda i: (i, 0)),
            pl.BlockSpec((1, D_MODEL), lambda i: (0, 0)),
        ],
        out_specs=pl.BlockSpec((tm, D_MODEL), lambda i: (i, 0)),
        compiler_params=_params(1),
        name="input_norm",
    )(x, gain)


SECTION = 1024
SECTIONS_PER_STEP = 1
N_SECTIONS = IN_WIDTH // SECTION
assert all(off % SECTION == 0 for off in (K_OFF, V_OFF, G_OFF, U_OFF, GATE_OFF))
assert RET_QK_WIDTH == SECTION and POOL_WIDTH == SECTION


def _inproj_kernel(*refs, tiles_per_seq, n_cast_src, n_cast_dst):
    xn_ref, w_ref, cos_ref, sin_ref, decay_ref, pmix_ref, pscale_ref = refs[:7]
    cast_src = refs[7:7 + n_cast_src]
    o_ref = refs[7 + n_cast_src]
    cast_dst = refs[8 + n_cast_src:8 + n_cast_src + n_cast_dst]
    halo_ref = refs[8 + n_cast_src + n_cast_dst]
    i = pl.program_id(0)
    j = pl.program_id(1)
    tm = xn_ref.shape[0]

    def blockwise(cols, epilogue):
        n_blocks = (cols.stop - cols.start) // MXU_COLS
        xn = xn_ref[...]
        for c in range(n_blocks):
            blk = slice(cols.start + c * MXU_COLS, cols.start + (c + 1) * MXU_COLS)
            acc = jnp.dot(xn, w_ref[:, blk], preferred_element_type=F32)
            o_ref[:, blk] = epilogue(acc, blk).astype(o_ref.dtype)
            _run_casts(cast_src, cast_dst, c, n_blocks)

    def rotary_section(cols, section):
        cos = cos_ref[...]
        sin = sin_ref[...]

        def epilogue(acc, blk):
            rot = _rotary(acc, cos, sin).reshape(tm // RET_CHUNK, RET_CHUNK, MXU_COLS)
            return (rot * decay_ref[section, :, blk]).reshape(tm, MXU_COLS)

        blockwise(cols, epilogue)

    def plain_section(cols):
        blockwise(cols, lambda acc, blk: acc)

    def silu_section(cols):
        blockwise(cols, lambda acc, blk: _silu(acc))

    def sigmoid_section(cols):
        blockwise(cols, lambda acc, blk: _sigmoid(acc))

    def pool_section(cols):
        seq_tile = i % tiles_per_seq

        @pl.when(seq_tile == 0)
        def _():
            halo_ref[...] = jnp.zeros_like(halo_ref)

        row = lax.broadcasted_iota(jnp.int32, (tm, 1), 0) + seq_tile * tm
        pos1 = (row + 1).astype(F32)
        xn = xn_ref[...]
        group = lambda gi: slice(gi * POOL_GROUP_DIM, (gi + 1) * POOL_GROUP_DIM)
        shifted = lambda gi: slice(cols.start + gi * POOL_GROUP_DIM,
                                   cols.start + (gi + 1) * POOL_GROUP_DIM)

        def pool_and_mix(gi, u):
            s = jnp.concatenate([halo_ref[:, group(gi)], u], axis=0)
            halo_ref[:, group(gi)] = u[tm - POOL_HALO:, :]
            shift = 1
            while shift < POOL_WINDOWS[gi]:
                s = s + pltpu.roll(s, shift, axis=0)
                shift *= 2
            count = jnp.minimum(pos1, float(POOL_WINDOWS[gi]))
            pooled = s[POOL_HALO:, :] / count - u
            mixed = jnp.dot(pooled.astype(BF16), pmix_ref[gi], preferred_element_type=F32)
            o_ref[:, shifted(gi)] = (mixed * pscale_ref[:, group(gi)]).astype(o_ref.dtype)

        order = sorted(range(POOL_GROUPS), key=lambda gi: -POOL_WINDOWS[gi])
        prev = None
        for n, gi in enumerate(order):
            u = jnp.dot(xn, w_ref[:, shifted(gi)], preferred_element_type=F32)
            if prev is not None:
                pool_and_mix(*prev)
            _run_casts(cast_src, cast_dst, n, POOL_GROUPS)
            prev = (gi, u)
        pool_and_mix(*prev)

    def section_fn(index):
        off = index * SECTION
        if off < V_OFF:
            return rotary_section
        if off < G_OFF:
            return plain_section
        if off < U_OFF:
            return silu_section
        if off == U_OFF:
            return pool_section
        return sigmoid_section

    n_steps = pl.cdiv(N_SECTIONS, SECTIONS_PER_STEP)
    step_fns = [tuple(section_fn(s * SECTIONS_PER_STEP + h) for h in range(SECTIONS_PER_STEP)
                      if s * SECTIONS_PER_STEP + h < N_SECTIONS) for s in range(n_steps)]
    lo = 0
    while lo < n_steps:
        hi = lo + 1
        while hi < n_steps and step_fns[hi] == step_fns[lo]:
            hi += 1

        @pl.when((j >= lo) & (j < hi))
        def _(fns=step_fns[lo]):
            for half, fn in enumerate(fns):
                cols = slice(half * SECTION, (half + 1) * SECTION)
                if fn is rotary_section:
                    fn(cols, j * SECTIONS_PER_STEP + half)
                else:
                    fn(cols)

        lo = hi


def _inproj(xn, w_in_b, cos, sin, qk_decay, pool_mix, pool_scale, layer, seq, casts_of):
    t = xn.shape[0]
    tm, tn = 1024, SECTION * SECTIONS_PER_STEP
    assert seq % tm == 0
    grid = (t // tm, pl.cdiv(IN_WIDTH, tn))
    casts = casts_of(grid)
    kern = functools.partial(_inproj_kernel, tiles_per_seq=seq // tm, n_cast_src=len(casts),
                             n_cast_dst=len(_cast_out_specs(casts)))
    return pl.pallas_call(
        kern,
        out_shape=[jax.ShapeDtypeStruct((t, IN_WIDTH), BF16)] + _cast_out_shapes(casts),
        grid=grid,
        in_specs=[
            pl.BlockSpec((tm, D_MODEL), lambda i, j: (i, 0)),
            pl.BlockSpec((D_MODEL, tn), lambda i, j: (0, j)),
            pl.BlockSpec((tm, RET_QK_DIM), lambda i, j: (i, 0)),
            pl.BlockSpec((tm, RET_QK_DIM), lambda i, j: (i, 0)),
            pl.BlockSpec(qk_decay.shape, lambda i, j: (0, 0, 0)),
            pl.BlockSpec((None, POOL_GROUPS, POOL_GROUP_DIM, POOL_GROUP_DIM),
                         lambda i, j: (layer, 0, 0, 0)),
            pl.BlockSpec((None, 1, POOL_WIDTH), lambda i, j: (layer, 0, 0)),
        ] + [c.in_spec for c in casts],
        out_specs=[pl.BlockSpec((tm, tn), lambda i, j: (i, j))] + _cast_out_specs(casts),
        scratch_shapes=[pltpu.VMEM((POOL_HALO, POOL_WIDTH), F32)],
        compiler_params=_params(2),
        name="in_proj",
    )(xn, w_in_b, cos, sin, qk_decay, pool_mix, pool_scale, *[c.stacked for c in casts])


def _retention_tables():
    c = RET_CHUNK
    log_gamma = jnp.log(1.0 - 2.0 ** (-5.0 - jnp.arange(RET_HEADS, dtype=F32)))
    idx = jnp.arange(c, dtype=F32)
    diff = idx[:, None] - idx[None, :]
    causal = jnp.where(diff[None] >= 0, jnp.exp(-c * log_gamma)[:, None, None], 0.0)
    xi = jnp.exp((idx + 1.0)[None, :] * log_gamma[:, None])
    zeta = jnp.exp((c - 1.0 - idx)[None, :] * log_gamma[:, None])
    chunk_decay = jnp.exp(c * log_gamma)
    per_head = lambda a: jnp.broadcast_to(a.T[:, :, None], (c, RET_HEADS, RET_QK_DIM)).reshape(
        c, RET_QK_WIDTH)
    qk_decay = jnp.stack([per_head(xi) * (RET_QK_DIM ** -0.5), per_head(zeta)])
    chunk_decay = jnp.broadcast_to(chunk_decay[:, None, None], (RET_HEADS, 1, RET_V_DIM))
    return qk_decay, causal, chunk_decay


def _retention_kernel(qx_ref, kz_ref, v_ref, g_ref, causal_ref, cdecay_ref, o_ref, state_ref):
    c = RET_CHUNK

    @pl.when(pl.program_id(1) == 0)
    def _():
        state_ref[...] = jnp.zeros_like(state_ref)

    heads = range(RET_HEADS)
    qk_cols = [slice(h * RET_QK_DIM, (h + 1) * RET_QK_DIM) for h in heads]
    v_cols = [slice(h * RET_V_DIM, (h + 1) * RET_V_DIM) for h in heads]
    for n in range(qx_ref.shape[0] // c):
        rows = slice(n * c, (n + 1) * c)
        qx = [qx_ref[rows, qk_cols[h]] for h in heads]
        kz = [kz_ref[rows, qk_cols[h]] for h in heads]
        v = [v_ref[rows, v_cols[h]] for h in heads]
        scores = [lax.dot_general(qx[h], kz[h], (((1,), (1,)), ((), ())),
                                  preferred_element_type=F32) for h in heads]
        kv = [lax.dot_general(kz[h], v[h], (((0,), (0,)), ((), ())),
                              preferred_element_type=F32) for h in heads]
        state = [state_ref[h] for h in heads]
        lhs = [jnp.concatenate([(scores[h] * causal_ref[h]).astype(BF16), qx[h]], axis=1)
               for h in heads]
        rhs = [jnp.concatenate([v[h], state[h].astype(BF16)], axis=0) for h in heads]
        y = [jnp.dot(lhs[h], rhs[h], preferred_element_type=F32) for h in heads]
        for h in heads:
            state_ref[h] = state[h] * cdecay_ref[h] + kv[h]
            yn = y[h] * lax.rsqrt(jnp.mean(y[h] * y[h], axis=-1, keepdims=True) + NORM_EPS)
            o_ref[rows, v_cols[h]] = yn.astype(o_ref.dtype) * g_ref[rows, v_cols[h]]


def _retention(proj, causal, cdecay, batch, seq):
    t = proj.shape[0]
    ts = 1024
    spb = seq // ts
    tok_blk = lambda width, off: pl.BlockSpec((ts, width), lambda b, s: (b * spb + s, off // width))
    full_blk = lambda a: pl.BlockSpec(a.shape, lambda b, s: (0,) * a.ndim)
    return pl.pallas_call(
        _retention_kernel,
        out_shape=jax.ShapeDtypeStruct((t, RET_V_WIDTH), BF16),
        grid=(batch, spb),
        in_specs=[
            tok_blk(RET_QK_WIDTH, Q_OFF), tok_blk(RET_QK_WIDTH, K_OFF),
            tok_blk(RET_V_WIDTH, V_OFF), tok_blk(RET_V_WIDTH, G_OFF),
            full_blk(causal), full_blk(cdecay),
        ],
        out_specs=tok_blk(RET_V_WIDTH, 0),
        scratch_shapes=[pltpu.VMEM((RET_HEADS, RET_QK_DIM, RET_V_DIM), F32)],
        compiler_params=_params(2),
        name="retention",
    )(proj, proj, proj, proj, causal, cdecay)


def _merge_kernel(yr_ref, yp_ref, g0_ref, g1_ref, wr_ref, wp_ref, o_ref):
    yr = yr_ref[...]
    yp = yp_ref[...]
    for c in range(o_ref.shape[1] // MXU_COLS):
        cols = slice(c * MXU_COLS, (c + 1) * MXU_COLS)
        r = jnp.dot(yr, wr_ref[:, cols], preferred_element_type=F32)
        p = jnp.dot(yp, wp_ref[:, cols], preferred_element_type=F32)
        merged = g0_ref[:, cols].astype(F32) * r + g1_ref[:, cols].astype(F32) * p
        o_ref[:, cols] = merged.astype(o_ref.dtype)


def _merge(y_ret, proj, w_ret_up_b, w_pool_up_b):
    t = y_ret.shape[0]
    tm, tn = 1024, 1024
    assert POOL_WIDTH == tn
    return pl.pallas_call(
        _merge_kernel,
        out_shape=jax.ShapeDtypeStruct((t, D_MODEL), BF16),
        grid=(t // tm, D_MODEL // tn),
        in_specs=[
            pl.BlockSpec((tm, RET_V_WIDTH), lambda i, j: (i, 0)),
            pl.BlockSpec((tm, POOL_WIDTH), lambda i, j: (i, U_OFF // POOL_WIDTH)),
            pl.BlockSpec((tm, tn), lambda i, j: (i, GATE_OFF // tn + j)),
            pl.BlockSpec((tm, tn), lambda i, j: (i, (GATE_OFF + D_MODEL) // tn + j)),
            pl.BlockSpec((RET_V_WIDTH, tn), lambda i, j: (0, j)),
            pl.BlockSpec((POOL_WIDTH, tn), lambda i, j: (0, j)),
        ],
        out_specs=pl.BlockSpec((tm, tn), lambda i, j: (i, j)),
        compiler_params=_params(2),
        name="merge",
    )(y_ret, proj, proj, proj, w_ret_up_b, w_pool_up_b)


def _residual_norm_kernel(*refs, emit_residual):
    x_ref, a_ref, w_ref, gain_ref = refs[:4]
    if emit_residual:
        res_ref, xn_ref = refs[4:]
    else:
        (xn_ref,) = refs[4:]
        res_ref = xn_ref
    a = a_ref[...]
    for c in range(res_ref.shape[1] // MXU_COLS):
        cols = slice(c * MXU_COLS, (c + 1) * MXU_COLS)
        res_ref[:, cols] = x_ref[:, cols] + jnp.dot(a, w_ref[:, cols], preferred_element_type=F32)
    _rmsnorm_rows(res_ref, gain_ref, xn_ref)


def _residual_norm(x, a, w_b, gain, tm, name, xn_dtype, emit_residual=True):
    t, k = a.shape
    row_blk = lambda: pl.BlockSpec((tm, D_MODEL), lambda i: (i, 0))
    out_shape = [jax.ShapeDtypeStruct(x.shape, xn_dtype)]
    out_specs = [row_blk()]
    if emit_residual:
        out_shape.insert(0, jax.ShapeDtypeStruct(x.shape, x.dtype))
        out_specs.insert(0, row_blk())
    return pl.pallas_call(
        functools.partial(_residual_norm_kernel, emit_residual=emit_residual),
        out_shape=out_shape,
        grid=(t // tm,),
        in_specs=[
            row_blk(),
            pl.BlockSpec((tm, k), lambda i: (i, 0)),
            pl.BlockSpec((k, D_MODEL), lambda i: (0, 0), pipeline_mode=pl.Buffered(1)),
            pl.BlockSpec((1, D_MODEL), lambda i: (0, 0)),
        ],
        out_specs=out_specs,
        compiler_params=_params(1),
        name=name,
    )(x, a, w_b, gain)


def _ffn_up_kernel(*refs, d_ff, n_cast_src, n_cast_dst):
    xn_ref, wg_ref, wu_ref = refs[:3]
    cast_src = refs[3:3 + n_cast_src]
    o_ref = refs[3 + n_cast_src]
    cast_dst = refs[4 + n_cast_src:4 + n_cast_src + n_cast_dst]
    j = pl.program_id(1)
    tn = o_ref.shape[1]

    def body(n_blocks):
        xn = xn_ref[...]
        for c in range(n_blocks):
            cols = slice(c * MXU_COLS, (c + 1) * MXU_COLS)
            gate = jnp.dot(xn, wg_ref[:, cols], preferred_element_type=F32)
            up = jnp.dot(xn, wu_ref[:, cols], preferred_element_type=F32)
            o_ref[:, cols] = (_silu(gate) * up).astype(o_ref.dtype)
            _run_casts(cast_src, cast_dst, c, n_blocks)

    full_steps, tail = divmod(d_ff, tn)

    @pl.when(j < full_steps)
    def _():
        body(tn // MXU_COLS)

    if tail:
        @pl.when(j == full_steps)
        def _():
            body(tail // MXU_COLS)


def _ffn_up(xn, w_g_b, w_u_b, casts_of):
    t = xn.shape[0]
    d_ff = w_g_b.shape[1]
    tm, tn = 1024, 1024
    assert d_ff % MXU_COLS == 0
    grid = (t // tm, pl.cdiv(d_ff, tn))
    casts = casts_of(grid)
    kern = functools.partial(_ffn_up_kernel, d_ff=d_ff, n_cast_src=len(casts),
                             n_cast_dst=len(_cast_out_specs(casts)))
    return pl.pallas_call(
        kern,
        out_shape=[jax.ShapeDtypeStruct((t, d_ff), BF16)] + _cast_out_shapes(casts),
        grid=grid,
        in_specs=[
            pl.BlockSpec((tm, D_MODEL), lambda i, j: (i, 0)),
            pl.BlockSpec((D_MODEL, tn), lambda i, j: (0, j)),
            pl.BlockSpec((D_MODEL, tn), lambda i, j: (0, j)),
        ] + [c.in_spec for c in casts],
        out_specs=[pl.BlockSpec((tm, tn), lambda i, j: (i, j))] + _cast_out_specs(casts),
        compiler_params=_params(2),
        name="ffn_up",
    )(xn, w_g_b, w_u_b, *[c.stacked for c in casts])


def kernel(x, positions, norm1, w_in, pool_mix, pool_scale, w_ret_up, w_pool_up, w_o,
           norm2, w_gu, w_down, norm_f):
    batch, seq, _ = x.shape
    depth = w_in.shape[0]
    t = batch * seq
    xt = x.reshape(t, D_MODEL)

    pool_mix_b = pool_mix.astype(BF16)
    pool_scale_r = pool_scale.reshape(depth, 1, POOL_WIDTH)

    cos, sin = _rope_tables(positions)
    qk_decay, causal, chunk_decay = _retention_tables()

    w_in_b = w_in[0].astype(BF16)
    xn = _norm(xt, norm1[0][None, :], BF16)
    mixer = (w_ret_up, w_pool_up, w_o)
    for layer in range(depth):
        last = layer + 1 == depth
        proj, w_g_b, w_u_b, w_down_b, *mixer_b = _inproj(
            xn, w_in_b, cos, sin, qk_decay, pool_mix_b, pool_scale_r, layer, seq,
            lambda grid: [_cast_stream(w_gu, layer, grid, n_split=2),
                          _cast_stream(w_down, layer, grid)]
            + [_cast_stream(w, layer, grid) for w in (mixer if layer == 0 else ())])
        if mixer_b:
            w_ret_up_b, w_pool_up_b, w_o_b = mixer_b
        y_ret = _retention(proj, causal, chunk_decay, batch, seq)
        merged = _merge(y_ret, proj, w_ret_up_b, w_pool_up_b)
        xt, xn = _residual_norm(xt, merged, w_o_b, norm2[layer][None, :], 512, "out_proj", BF16)
        hidden, *next_b = _ffn_up(
            xn, w_g_b, w_u_b,
            lambda grid: [] if last else [
                _cast_stream(w, layer + 1, grid) for w in (w_in,) + mixer])
        if last:
            (out,) = _residual_norm(xt, hidden, w_down_b, norm_f[None, :], 512, "ffn_down", F32,
                                    emit_residual=False)
        else:
            w_in_b, w_ret_up_b, w_pool_up_b, w_o_b = next_b
            xt, xn = _residual_norm(xt, hidden, w_down_b, norm1[layer + 1][None, :], 512,
                                    "ffn_down", BF16)
    return out.reshape(batch, seq, D_MODEL)
```

```python
import functools
from typing import NamedTuple

import jax
import jax.numpy as jnp
from jax import lax
from jax.experimental import pallas as pl
from jax.experimental.pallas import tpu as pltpu

D_MODEL = 2048
RET_HEADS = 8
RET_QK_DIM = 128
RET_V_DIM = 256
RET_QK_WIDTH = RET_HEADS * RET_QK_DIM
RET_V_WIDTH = RET_HEADS * RET_V_DIM
RET_CHUNK = 128
ROPE_BASE = 10000.0
POOL_WINDOWS = (2, 4, 8, 16)
POOL_GROUPS = len(POOL_WINDOWS)
POOL_GROUP_DIM = D_MODEL // 8
POOL_WIDTH = POOL_GROUPS * POOL_GROUP_DIM
N_BRANCHES = 2
IN_WIDTH = 2 * RET_QK_WIDTH + 2 * RET_V_WIDTH + POOL_WIDTH + N_BRANCHES * D_MODEL
NORM_EPS = 1e-6

Q_OFF = 0
K_OFF = RET_QK_WIDTH
V_OFF = 2 * RET_QK_WIDTH
G_OFF = V_OFF + RET_V_WIDTH
U_OFF = G_OFF + RET_V_WIDTH
GATE_OFF = U_OFF + POOL_WIDTH

POOL_HALO = 16
assert POOL_HALO >= max(POOL_WINDOWS) and POOL_HALO % 8 == 0

VMEM_LIMIT_BYTES = 56 * 1024 * 1024
MXU_COLS = 256
assert POOL_GROUP_DIM == MXU_COLS

BF16 = jnp.bfloat16
F32 = jnp.float32


def _params(n_axes):
    return pltpu.CompilerParams(
        dimension_semantics=("arbitrary",) * n_axes,
        vmem_limit_bytes=VMEM_LIMIT_BYTES,
    )


BF16_SUBLANES = 16


class _CastStream(NamedTuple):
    stacked: jax.Array
    in_spec: pl.BlockSpec
    out_specs: tuple
    out_shapes: tuple


def _cast_stream(stacked, layer, grid, n_split=1):
    _, k, n = stacked.shape
    n_steps = grid[0] * grid[1]
    rows = BF16_SUBLANES
    while k % rows or k // rows > n_steps:
        rows += BF16_SUBLANES
    last = k // rows - 1
    block = lambda i, j: jnp.minimum(i * grid[1] + j, last)
    return _CastStream(
        stacked,
        pl.BlockSpec((None, rows, n), lambda i, j: (layer, block(i, j), 0)),
        (pl.BlockSpec((rows, n // n_split), lambda i, j: (block(i, j), 0)),) * n_split,
        (jax.ShapeDtypeStruct((k, n // n_split), BF16),) * n_split,
    )


def _cast_out_specs(casts):
    return [s for c in casts for s in c.out_specs]


def _cast_out_shapes(casts):
    return [s for c in casts for s in c.out_shapes]


def _run_casts(src_refs, dst_refs, part=0, n_parts=1):
    d = 0
    for src in src_refs:
        col = 0
        while col < src.shape[1]:
            dst = dst_refs[d]
            width = dst.shape[1] // n_parts
            lo = part * width
            dst[:, lo:lo + width] = src[:, col + lo:col + lo + width].astype(dst.dtype)
            col += dst.shape[1]
            d += 1
    assert d == len(dst_refs)


def _sigmoid(x):
    return 0.5 * jnp.tanh(0.5 * x) + 0.5


def _silu(x):
    h = 0.5 * x
    return h + h * jnp.tanh(h)


def _rotary(t, cos, sin):
    outs = []
    for h in range(t.shape[1] // RET_QK_DIM):
        th = t[:, h * RET_QK_DIM:(h + 1) * RET_QK_DIM]
        outs.append(th * cos + pltpu.roll(th, RET_QK_DIM // 2, axis=1) * sin)
    return jnp.concatenate(outs, axis=1)


def _col_blocks(width):
    return [slice(c, c + MXU_COLS) for c in range(0, width, MXU_COLS)]


def _rmsnorm_rows(x_ref, gain_ref, out_ref):
    n_blocks = x_ref.shape[1] // MXU_COLS
    col = lambda c: slice(c * MXU_COLS, (c + 1) * MXU_COLS)
    sq = None
    for c in range(n_blocks):
        xc = x_ref[:, col(c)]
        sq = xc * xc if sq is None else sq + xc * xc
    scale = lax.rsqrt(jnp.sum(sq, axis=-1, keepdims=True) / x_ref.shape[1] + NORM_EPS)
    for c in range(n_blocks):
        out_ref[:, col(c)] = (x_ref[:, col(c)] * scale * gain_ref[:, col(c)]).astype(out_ref.dtype)


def _input_kernel(x_ref, gain_ref, pos_ref, freq_ref, xn_ref, cos_ref, sin_ref):
    _rmsnorm_rows(x_ref, gain_ref, xn_ref)
    ang = pos_ref[...] * freq_ref[...]
    lane = lax.broadcasted_iota(jnp.int32, ang.shape, 1)
    sin = jnp.sin(ang)
    cos_ref[...] = jnp.cos(ang)
    sin_ref[...] = jnp.where(lane < RET_QK_DIM // 2, -sin, sin)


def _input_stage(x, gain, positions):
    t = x.shape[0]
    half = RET_QK_DIM // 2
    inv_freq = ROPE_BASE ** (-jnp.arange(half, dtype=F32) / half)
    freq = jnp.concatenate([inv_freq, inv_freq])[None, :]
    pos = jnp.broadcast_to(positions.reshape(t, 1).astype(F32), (t, RET_QK_DIM))
    tm = 512
    rows = lambda width: pl.BlockSpec((tm, width), lambda i: (i, 0))
    whole = lambda width: pl.BlockSpec((1, width), lambda i: (0, 0))
    return pl.pallas_call(
        _input_kernel,
        out_shape=[jax.ShapeDtypeStruct(x.shape, BF16),
                   jax.ShapeDtypeStruct((t, RET_QK_DIM), F32),
                   jax.ShapeDtypeStruct((t, RET_QK_DIM), F32)],
        grid=(t // tm,),
        in_specs=[rows(D_MODEL), whole(D_MODEL), rows(RET_QK_DIM), whole(RET_QK_DIM)],
        out_specs=[rows(D_MODEL), rows(RET_QK_DIM), rows(RET_QK_DIM)],
        compiler_params=_params(1),
        name="input_stage",
    )(x, gain, pos, freq)


SECTION = 1024
SECTIONS_PER_STEP = 1
N_SECTIONS = IN_WIDTH // SECTION
assert all(off % SECTION == 0 for off in (K_OFF, V_OFF, G_OFF, U_OFF, GATE_OFF))
assert RET_QK_WIDTH == SECTION and POOL_WIDTH == SECTION


def _inproj_kernel(*refs, tiles_per_seq, n_cast_src, n_cast_dst):
    xn_ref, w_ref, cos_ref, sin_ref, decay_ref, pmix_ref, pscale_ref = refs[:7]
    cast_src = refs[7:7 + n_cast_src]
    o_ref = refs[7 + n_cast_src]
    cast_dst = refs[8 + n_cast_src:8 + n_cast_src + n_cast_dst]
    halo_ref = refs[8 + n_cast_src + n_cast_dst]
    i = pl.program_id(0)
    j = pl.program_id(1)
    tm = xn_ref.shape[0]

    def blockwise(cols, epilogue):
        n_blocks = (cols.stop - cols.start) // MXU_COLS
        xn = xn_ref[...]
        for c in range(n_blocks):
            blk = slice(cols.start + c * MXU_COLS, cols.start + (c + 1) * MXU_COLS)
            acc = jnp.dot(xn, w_ref[:, blk], preferred_element_type=F32)
            o_ref[:, blk] = epilogue(acc, blk).astype(o_ref.dtype)
            _run_casts(cast_src, cast_dst, c, n_blocks)

    def rotary_section(cols, section):
        cos = cos_ref[...]
        sin = sin_ref[...]

        def epilogue(acc, blk):
            rot = _rotary(acc, cos, sin).reshape(tm // RET_CHUNK, RET_CHUNK, MXU_COLS)
            return (rot * decay_ref[section, :, blk]).reshape(tm, MXU_COLS)

        blockwise(cols, epilogue)

    def plain_section(cols):
        blockwise(cols, lambda acc, blk: acc)

    def silu_section(cols):
        blockwise(cols, lambda acc, blk: _silu(acc))

    def sigmoid_section(cols):
        blockwise(cols, lambda acc, blk: _sigmoid(acc))

    def pool_section(cols):
        seq_tile = i % tiles_per_seq

        @pl.when(seq_tile == 0)
        def _():
            halo_ref[...] = jnp.zeros_like(halo_ref)

        row = lax.broadcasted_iota(jnp.int32, (tm, 1), 0) + seq_tile * tm
        pos1 = (row + 1).astype(F32)
        xn = xn_ref[...]
        group = lambda gi: slice(gi * POOL_GROUP_DIM, (gi + 1) * POOL_GROUP_DIM)
        shifted = lambda gi: slice(cols.start + gi * POOL_GROUP_DIM,
                                   cols.start + (gi + 1) * POOL_GROUP_DIM)

        def pool_and_mix(gi, u):
            s = jnp.concatenate([halo_ref[:, group(gi)], u], axis=0)
            halo_ref[:, group(gi)] = u[tm - POOL_HALO:, :]
            shift = 1
            while shift < POOL_WINDOWS[gi]:
                s = s + pltpu.roll(s, shift, axis=0)
                shift *= 2
            count = jnp.minimum(pos1, float(POOL_WINDOWS[gi]))
            pooled = s[POOL_HALO:, :] / count - u
            mixed = jnp.dot(pooled.astype(BF16), pmix_ref[gi], preferred_element_type=F32)
            o_ref[:, shifted(gi)] = (mixed * pscale_ref[:, group(gi)]).astype(o_ref.dtype)

        order = sorted(range(POOL_GROUPS), key=lambda gi: -POOL_WINDOWS[gi])
        prev = None
        for n, gi in enumerate(order):
            u = jnp.dot(xn, w_ref[:, shifted(gi)], preferred_element_type=F32)
            if prev is not None:
                pool_and_mix(*prev)
            _run_casts(cast_src, cast_dst, n, POOL_GROUPS)
            prev = (gi, u)
        pool_and_mix(*prev)

    def section_fn(index):
        off = index * SECTION
        if off < V_OFF:
            return rotary_section
        if off < G_OFF:
            return plain_section
        if off < U_OFF:
            return silu_section
        if off == U_OFF:
            return pool_section
        return sigmoid_section

    n_steps = pl.cdiv(N_SECTIONS, SECTIONS_PER_STEP)
    step_fns = [tuple(section_fn(s * SECTIONS_PER_STEP + h) for h in range(SECTIONS_PER_STEP)
                      if s * SECTIONS_PER_STEP + h < N_SECTIONS) for s in range(n_steps)]
    lo = 0
    while lo < n_steps:
        hi = lo + 1
        while hi < n_steps and step_fns[hi] == step_fns[lo]:
            hi += 1

        @pl.when((j >= lo) & (j < hi))
        def _(fns=step_fns[lo]):
            for half, fn in enumerate(fns):
                cols = slice(half * SECTION, (half + 1) * SECTION)
                if fn is rotary_section:
                    fn(cols, j * SECTIONS_PER_STEP + half)
                else:
                    fn(cols)

        lo = hi


def _inproj(xn, w_in_b, cos, sin, qk_decay, pool_mix, pool_scale, layer, seq, casts_of):
    t = xn.shape[0]
    tm, tn = 1024, SECTION * SECTIONS_PER_STEP
    assert seq % tm == 0
    grid = (t // tm, pl.cdiv(IN_WIDTH, tn))
    casts = casts_of(grid)
    kern = functools.partial(_inproj_kernel, tiles_per_seq=seq // tm, n_cast_src=len(casts),
                             n_cast_dst=len(_cast_out_specs(casts)))
    return pl.pallas_call(
        kern,
        out_shape=[jax.ShapeDtypeStruct((t, IN_WIDTH), BF16)] + _cast_out_shapes(casts),
        grid=grid,
        in_specs=[
            pl.BlockSpec((tm, D_MODEL), lambda i, j: (i, 0)),
            pl.BlockSpec((D_MODEL, tn), lambda i, j: (0, j)),
            pl.BlockSpec((tm, RET_QK_DIM), lambda i, j: (i, 0)),
            pl.BlockSpec((tm, RET_QK_DIM), lambda i, j: (i, 0)),
            pl.BlockSpec(qk_decay.shape, lambda i, j: (0, 0, 0)),
            pl.BlockSpec((None, POOL_GROUPS, POOL_GROUP_DIM, POOL_GROUP_DIM),
                         lambda i, j: (layer, 0, 0, 0)),
            pl.BlockSpec((None, 1, POOL_WIDTH), lambda i, j: (layer, 0, 0)),
        ] + [c.in_spec for c in casts],
        out_specs=[pl.BlockSpec((tm, tn), lambda i, j: (i, j))] + _cast_out_specs(casts),
        scratch_shapes=[pltpu.VMEM((POOL_HALO, POOL_WIDTH), F32)],
        compiler_params=_params(2),
        name="in_proj",
    )(xn, w_in_b, cos, sin, qk_decay, pool_mix, pool_scale, *[c.stacked for c in casts])


def _retention_tables():
    c = RET_CHUNK
    log_gamma = jnp.log(1.0 - 2.0 ** (-5.0 - jnp.arange(RET_HEADS, dtype=F32)))
    idx = jnp.arange(c, dtype=F32)
    diff = idx[:, None] - idx[None, :]
    causal = jnp.where(diff[None] >= 0, jnp.exp(-c * log_gamma)[:, None, None], 0.0)
    xi = jnp.exp((idx + 1.0)[None, :] * log_gamma[:, None])
    zeta = jnp.exp((c - 1.0 - idx)[None, :] * log_gamma[:, None])
    chunk_decay = jnp.exp(c * log_gamma)
    per_head = lambda a: jnp.broadcast_to(a.T[:, :, None], (c, RET_HEADS, RET_QK_DIM)).reshape(
        c, RET_QK_WIDTH)
    qk_decay = jnp.stack([per_head(xi) * (RET_QK_DIM ** -0.5), per_head(zeta)])
    chunk_decay = jnp.broadcast_to(chunk_decay[:, None, None], (RET_HEADS, 1, RET_V_DIM))
    return qk_decay, causal, chunk_decay


GATE_BLOCK = 1024
assert GATE_OFF % GATE_BLOCK == 0 and D_MODEL % GATE_BLOCK == 0 and GATE_BLOCK % MXU_COLS == 0


def _mixer_kernel(*refs):
    qx_ref, kz_ref, v_ref, g_ref, yp_ref = refs[:5]
    n_gate = D_MODEL // GATE_BLOCK
    g0_refs = refs[5:5 + n_gate]
    g1_refs = refs[5 + n_gate:5 + 2 * n_gate]
    (causal_ref, cdecay_ref, wr_ref, wp_ref, o_ref,
     state_ref, yret_ref, pool_ref) = refs[5 + 2 * n_gate:]
    c = RET_CHUNK

    @pl.when(pl.program_id(1) == 0)
    def _():
        state_ref[...] = jnp.zeros_like(state_ref)

    def gate(gate_refs, cols):
        ref = gate_refs[cols.start // GATE_BLOCK]
        lo = cols.start % GATE_BLOCK
        return ref[:, lo:lo + MXU_COLS].astype(F32)

    out_blocks = _col_blocks(D_MODEL)
    n_chunks = qx_ref.shape[0] // c
    pool_blocks_per_chunk = pl.cdiv(len(out_blocks), n_chunks)
    yp = yp_ref[...]
    pending = list(out_blocks)

    def pool_branch(n_blocks):
        for _ in range(min(n_blocks, len(pending))):
            cols = pending.pop(0)
            p = jnp.dot(yp, wp_ref[:, cols], preferred_element_type=F32)
            pool_ref[:, cols] = gate(g1_refs, cols) * p

    heads = range(RET_HEADS)
    qk_cols = [slice(h * RET_QK_DIM, (h + 1) * RET_QK_DIM) for h in heads]
    v_cols = [slice(h * RET_V_DIM, (h + 1) * RET_V_DIM) for h in heads]
    for n in range(n_chunks):
        rows = slice(n * c, (n + 1) * c)
        qx = [qx_ref[rows, qk_cols[h]] for h in heads]
        kz = [kz_ref[rows, qk_cols[h]] for h in heads]
        v = [v_ref[rows, v_cols[h]] for h in heads]
        scores = [lax.dot_general(qx[h], kz[h], (((1,), (1,)), ((), ())),
                                  preferred_element_type=F32) for h in heads]
        kv = [lax.dot_general(kz[h], v[h], (((0,), (0,)), ((), ())),
                              preferred_element_type=F32) for h in heads]
        pool_branch(pool_blocks_per_chunk)
        state = [state_ref[h] for h in heads]
        lhs = [jnp.concatenate([(scores[h] * causal_ref[h]).astype(BF16), qx[h]], axis=1)
               for h in heads]
        rhs = [jnp.concatenate([v[h], state[h].astype(BF16)], axis=0) for h in heads]
        y = [jnp.dot(lhs[h], rhs[h], preferred_element_type=F32) for h in heads]
        for h in heads:
            state_ref[h] = state[h] * cdecay_ref[h] + kv[h]
            yn = y[h] * lax.rsqrt(jnp.mean(y[h] * y[h], axis=-1, keepdims=True) + NORM_EPS)
            yret_ref[rows, v_cols[h]] = yn.astype(yret_ref.dtype) * g_ref[rows, v_cols[h]]
    pool_branch(len(pending))

    yret = yret_ref[...]
    for cols in out_blocks:
        r = jnp.dot(yret, wr_ref[:, cols], preferred_element_type=F32)
        o_ref[:, cols] = (gate(g0_refs, cols) * r + pool_ref[:, cols]).astype(o_ref.dtype)


def _mixer(proj, causal, cdecay, w_ret_up_b, w_pool_up_b, batch, seq):
    t = proj.shape[0]
    ts = 512
    spb = seq // ts
    tok_blk = lambda width, off: pl.BlockSpec((ts, width), lambda b, s: (b * spb + s, off // width))
    full_blk = lambda a: pl.BlockSpec(a.shape, lambda b, s: (0,) * a.ndim)
    weight_blk = lambda a: pl.BlockSpec(a.shape, lambda b, s: (0, 0), pipeline_mode=pl.Buffered(1))
    gate_blks = [tok_blk(GATE_BLOCK, GATE_OFF + branch * D_MODEL + part * GATE_BLOCK)
                 for branch in range(N_BRANCHES) for part in range(D_MODEL // GATE_BLOCK)]
    return pl.pallas_call(
        _mixer_kernel,
        out_shape=jax.ShapeDtypeStruct((t, D_MODEL), BF16),
        grid=(batch, spb),
        in_specs=[
            tok_blk(RET_QK_WIDTH, Q_OFF), tok_blk(RET_QK_WIDTH, K_OFF),
            tok_blk(RET_V_WIDTH, V_OFF), tok_blk(RET_V_WIDTH, G_OFF),
            tok_blk(POOL_WIDTH, U_OFF),
            *gate_blks,
            full_blk(causal), full_blk(cdecay),
            weight_blk(w_ret_up_b), weight_blk(w_pool_up_b),
        ],
        out_specs=tok_blk(D_MODEL, 0),
        scratch_shapes=[
            pltpu.VMEM((RET_HEADS, RET_QK_DIM, RET_V_DIM), F32),
            pltpu.VMEM((ts, RET_V_WIDTH), BF16),
            pltpu.VMEM((ts, D_MODEL), F32),
        ],
        compiler_params=_params(2),
        name="mixer",
    )(proj, proj, proj, proj, proj, *([proj] * len(gate_blks)),
      causal, cdecay, w_ret_up_b, w_pool_up_b)


def _residual_norm_kernel(*refs, emit_residual):
    x_ref, a_ref, w_ref, gain_ref = refs[:4]
    if emit_residual:
        res_ref, xn_ref = refs[4:]
    else:
        (xn_ref,) = refs[4:]
        res_ref = xn_ref
    a = a_ref[...]
    for c in range(res_ref.shape[1] // MXU_COLS):
        cols = slice(c * MXU_COLS, (c + 1) * MXU_COLS)
        res_ref[:, cols] = x_ref[:, cols] + jnp.dot(a, w_ref[:, cols], preferred_element_type=F32)
    _rmsnorm_rows(res_ref, gain_ref, xn_ref)


def _residual_norm(x, a, w_b, gain, tm, name, xn_dtype, emit_residual=True):
    t, k = a.shape
    row_blk = lambda: pl.BlockSpec((tm, D_MODEL), lambda i: (i, 0))
    out_shape = [jax.ShapeDtypeStruct(x.shape, xn_dtype)]
    out_specs = [row_blk()]
    if emit_residual:
        out_shape.insert(0, jax.ShapeDtypeStruct(x.shape, x.dtype))
        out_specs.insert(0, row_blk())
    return pl.pallas_call(
        functools.partial(_residual_norm_kernel, emit_residual=emit_residual),
        out_shape=out_shape,
        grid=(t // tm,),
        in_specs=[
            row_blk(),
            pl.BlockSpec((tm, k), lambda i: (i, 0)),
            pl.BlockSpec((k, D_MODEL), lambda i: (0, 0), pipeline_mode=pl.Buffered(1)),
            pl.BlockSpec((1, D_MODEL), lambda i: (0, 0)),
        ],
        out_specs=out_specs,
        compiler_params=_params(1),
        name=name,
    )(x, a, w_b, gain)


def _ffn_up_kernel(*refs, d_ff, n_cast_src, n_cast_dst):
    xn_ref, wg_ref, wu_ref = refs[:3]
    cast_src = refs[3:3 + n_cast_src]
    o_ref = refs[3 + n_cast_src]
    cast_dst = refs[4 + n_cast_src:4 + n_cast_src + n_cast_dst]
    j = pl.program_id(1)
    tn = o_ref.shape[1]

    def body(n_blocks):
        xn = xn_ref[...]
        for c in range(n_blocks):
            cols = slice(c * MXU_COLS, (c + 1) * MXU_COLS)
            gate = jnp.dot(xn, wg_ref[:, cols], preferred_element_type=F32)
            up = jnp.dot(xn, wu_ref[:, cols], preferred_element_type=F32)
            o_ref[:, cols] = (_silu(gate) * up).astype(o_ref.dtype)
            _run_casts(cast_src, cast_dst, c, n_blocks)

    full_steps, tail = divmod(d_ff, tn)

    @pl.when(j < full_steps)
    def _():
        body(tn // MXU_COLS)

    if tail:
        @pl.when(j == full_steps)
        def _():
            body(tail // MXU_COLS)


def _ffn_up(xn, w_g_b, w_u_b, casts_of):
    t = xn.shape[0]
    d_ff = w_g_b.shape[1]
    tm, tn = 1024, 1024
    assert d_ff % MXU_COLS == 0
    grid = (t // tm, pl.cdiv(d_ff, tn))
    casts = casts_of(grid)
    kern = functools.partial(_ffn_up_kernel, d_ff=d_ff, n_cast_src=len(casts),
                             n_cast_dst=len(_cast_out_specs(casts)))
    return pl.pallas_call(
        kern,
        out_shape=[jax.ShapeDtypeStruct((t, d_ff), BF16)] + _cast_out_shapes(casts),
        grid=grid,
        in_specs=[
            pl.BlockSpec((tm, D_MODEL), lambda i, j: (i, 0)),
            pl.BlockSpec((D_MODEL, tn), lambda i, j: (0, j)),
            pl.BlockSpec((D_MODEL, tn), lambda i, j: (0, j)),
        ] + [c.in_spec for c in casts],
        out_specs=[pl.BlockSpec((tm, tn), lambda i, j: (i, j))] + _cast_out_specs(casts),
        compiler_params=_params(2),
        name="ffn_up",
    )(xn, w_g_b, w_u_b, *[c.stacked for c in casts])


def kernel(x, positions, norm1, w_in, pool_mix, pool_scale, w_ret_up, w_pool_up, w_o,
           norm2, w_gu, w_down, norm_f):
    batch, seq, _ = x.shape
    depth = w_in.shape[0]
    t = batch * seq
    xt = x.reshape(t, D_MODEL)

    pool_mix_b = pool_mix.astype(BF16)
    pool_scale_r = pool_scale.reshape(depth, 1, POOL_WIDTH)

    qk_decay, causal, chunk_decay = _retention_tables()

    w_in_b = w_in[0].astype(BF16)
    xn, cos, sin = _input_stage(xt, norm1[0][None, :], positions)
    mixer = (w_ret_up, w_pool_up, w_o)
    for layer in range(depth):
        last = layer + 1 == depth
        proj, w_g_b, w_u_b, w_down_b, *mixer_b = _inproj(
            xn, w_in_b, cos, sin, qk_decay, pool_mix_b, pool_scale_r, layer, seq,
            lambda grid: [_cast_stream(w_gu, layer, grid, n_split=2),
                          _cast_stream(w_down, layer, grid)]
            + [_cast_stream(w, layer, grid) for w in (mixer if layer == 0 else ())])
        if mixer_b:
            w_ret_up_b, w_pool_up_b, w_o_b = mixer_b
        merged = _mixer(proj, causal, chunk_decay, w_ret_up_b, w_pool_up_b, batch, seq)
        xt, xn = _residual_norm(xt, merged, w_o_b, norm2[layer][None, :], 512, "out_proj", BF16)
        hidden, *next_b = _ffn_up(
            xn, w_g_b, w_u_b,
            lambda grid: [] if last else [
                _cast_stream(w, layer + 1, grid) for w in (w_in,) + mixer])
        if last:
            (out,) = _residual_norm(xt, hidden, w_down_b, norm_f[None, :], 512, "ffn_down", F32,
                                    emit_residual=False)
        else:
            w_in_b, w_ret_up_b, w_pool_up_b, w_o_b = next_b
            xt, xn = _residual_norm(xt, hidden, w_down_b, norm1[layer + 1][None, :], 512,
                                    "ffn_down", BF16)
    return out.reshape(batch, seq, D_MODEL)
```

```python
import functools
from typing import NamedTuple

import jax
import jax.numpy as jnp
from jax import lax
from jax.experimental import pallas as pl
from jax.experimental.pallas import tpu as pltpu

D_MODEL = 2048
RET_HEADS = 8
RET_QK_DIM = 128
RET_V_DIM = 256
RET_QK_WIDTH = RET_HEADS * RET_QK_DIM
RET_V_WIDTH = RET_HEADS * RET_V_DIM
RET_CHUNK = 128
ROPE_BASE = 10000.0
POOL_WINDOWS = (2, 4, 8, 16)
POOL_GROUPS = len(POOL_WINDOWS)
POOL_GROUP_DIM = D_MODEL // 8
POOL_WIDTH = POOL_GROUPS * POOL_GROUP_DIM
N_BRANCHES = 2
IN_WIDTH = 2 * RET_QK_WIDTH + 2 * RET_V_WIDTH + POOL_WIDTH + N_BRANCHES * D_MODEL
NORM_EPS = 1e-6

Q_OFF = 0
K_OFF = RET_QK_WIDTH
V_OFF = 2 * RET_QK_WIDTH
G_OFF = V_OFF + RET_V_WIDTH
U_OFF = G_OFF + RET_V_WIDTH
GATE_OFF = U_OFF + POOL_WIDTH

POOL_HALO = 16
assert POOL_HALO >= max(POOL_WINDOWS) and POOL_HALO % 8 == 0

VMEM_LIMIT_BYTES = 56 * 1024 * 1024
MXU_COLS = 256
assert POOL_GROUP_DIM == MXU_COLS

BF16 = jnp.bfloat16
F32 = jnp.float32


def _params(n_axes):
    return pltpu.CompilerParams(
        dimension_semantics=("arbitrary",) * n_axes,
        vmem_limit_bytes=VMEM_LIMIT_BYTES,
    )


BF16_SUBLANES = 16


class _CastStream(NamedTuple):
    stacked: jax.Array
    in_spec: pl.BlockSpec
    out_specs: tuple
    out_shapes: tuple


def _cast_stream(stacked, layer, grid, n_split=1):
    _, k, n = stacked.shape
    n_steps = grid[0] * grid[1]
    rows = BF16_SUBLANES
    while k % rows or k // rows > n_steps:
        rows += BF16_SUBLANES
    last = k // rows - 1
    block = lambda i, j: jnp.minimum(i * grid[1] + j, last)
    return _CastStream(
        stacked,
        pl.BlockSpec((None, rows, n), lambda i, j: (layer, block(i, j), 0)),
        (pl.BlockSpec((rows, n // n_split), lambda i, j: (block(i, j), 0)),) * n_split,
        (jax.ShapeDtypeStruct((k, n // n_split), BF16),) * n_split,
    )


def _cast_out_specs(casts):
    return [s for c in casts for s in c.out_specs]


def _cast_out_shapes(casts):
    return [s for c in casts for s in c.out_shapes]


def _run_casts(src_refs, dst_refs, part=0, n_parts=1):
    d = 0
    for src in src_refs:
        col = 0
        while col < src.shape[1]:
            dst = dst_refs[d]
            width = dst.shape[1] // n_parts
            lo = part * width
            dst[:, lo:lo + width] = src[:, col + lo:col + lo + width].astype(dst.dtype)
            col += dst.shape[1]
            d += 1
    assert d == len(dst_refs)


def _sigmoid(x):
    return 0.5 * jnp.tanh(0.5 * x) + 0.5


def _silu(x):
    h = 0.5 * x
    return h + h * jnp.tanh(h)


def _rotary(t, cos, sin):
    outs = []
    for h in range(t.shape[1] // RET_QK_DIM):
        th = t[:, h * RET_QK_DIM:(h + 1) * RET_QK_DIM]
        outs.append(th * cos + pltpu.roll(th, RET_QK_DIM // 2, axis=1) * sin)
    return jnp.concatenate(outs, axis=1)


def _col_blocks(width):
    return [slice(c, c + MXU_COLS) for c in range(0, width, MXU_COLS)]


def _rmsnorm_rows(x_ref, gain_ref, out_ref):
    n_blocks = x_ref.shape[1] // MXU_COLS
    col = lambda c: slice(c * MXU_COLS, (c + 1) * MXU_COLS)
    sq = None
    for c in range(n_blocks):
        xc = x_ref[:, col(c)]
        sq = xc * xc if sq is None else sq + xc * xc
    scale = lax.rsqrt(jnp.sum(sq, axis=-1, keepdims=True) / x_ref.shape[1] + NORM_EPS)
    for c in range(n_blocks):
        out_ref[:, col(c)] = (x_ref[:, col(c)] * scale * gain_ref[:, col(c)]).astype(out_ref.dtype)


def _input_kernel(x_ref, gain_ref, pos_ref, freq_ref, xn_ref, cos_ref, sin_ref):
    _rmsnorm_rows(x_ref, gain_ref, xn_ref)
    half_rows = pos_ref.shape[0]
    ang = pos_ref[...] * freq_ref[...]
    lane = lax.broadcasted_iota(jnp.int32, ang.shape, 1)
    low = lane < RET_QK_DIM // 2
    cos = jnp.cos(ang)
    sin = jnp.sin(ang)
    cos_swapped = pltpu.roll(cos, RET_QK_DIM // 2, axis=1)
    sin_swapped = pltpu.roll(sin, RET_QK_DIM // 2, axis=1)
    cos_ref[:half_rows, :] = jnp.where(low, cos, cos_swapped)
    sin_ref[:half_rows, :] = jnp.where(low, -sin, sin_swapped)
    cos_ref[half_rows:, :] = jnp.where(low, cos_swapped, cos)
    sin_ref[half_rows:, :] = jnp.where(low, -sin_swapped, sin)


def _input_stage(x, gain, positions):
    t = x.shape[0]
    tm = 512
    half = RET_QK_DIM // 2
    inv_freq = ROPE_BASE ** (-jnp.arange(half, dtype=F32) / half)
    freq = jnp.concatenate([inv_freq, inv_freq])[None, :]
    pos = positions.reshape(t // tm, 2, tm // 2).astype(F32).transpose(0, 2, 1)
    pos = jnp.broadcast_to(pos[..., None], (t // tm, tm // 2, 2, half)).reshape(t // 2, RET_QK_DIM)
    rows = lambda n, width: pl.BlockSpec((n, width), lambda i: (i, 0))
    whole = lambda width: pl.BlockSpec((1, width), lambda i: (0, 0))
    return pl.pallas_call(
        _input_kernel,
        out_shape=[jax.ShapeDtypeStruct(x.shape, BF16),
                   jax.ShapeDtypeStruct((t, RET_QK_DIM), F32),
                   jax.ShapeDtypeStruct((t, RET_QK_DIM), F32)],
        grid=(t // tm,),
        in_specs=[rows(tm, D_MODEL), whole(D_MODEL), rows(tm // 2, RET_QK_DIM), whole(RET_QK_DIM)],
        out_specs=[rows(tm, D_MODEL), rows(tm, RET_QK_DIM), rows(tm, RET_QK_DIM)],
        compiler_params=_params(1),
        name="input_stage",
    )(x, gain, pos, freq)


SECTION = 1024
SECTIONS_PER_STEP = 1
N_SECTIONS = IN_WIDTH // SECTION
assert all(off % SECTION == 0 for off in (K_OFF, V_OFF, G_OFF, U_OFF, GATE_OFF))
assert RET_QK_WIDTH == SECTION and POOL_WIDTH == SECTION


def _inproj_kernel(*refs, tiles_per_seq, n_cast_src, n_cast_dst):
    xn_ref, w_ref, cos_ref, sin_ref, decay_ref, pmix_ref, pscale_ref = refs[:7]
    cast_src = refs[7:7 + n_cast_src]
    o_ref = refs[7 + n_cast_src]
    cast_dst = refs[8 + n_cast_src:8 + n_cast_src + n_cast_dst]
    halo_ref = refs[8 + n_cast_src + n_cast_dst]
    i = pl.program_id(0)
    j = pl.program_id(1)
    tm = xn_ref.shape[0]

    def blockwise(cols, epilogue):
        n_blocks = (cols.stop - cols.start) // MXU_COLS
        xn = xn_ref[...]
        for c in range(n_blocks):
            blk = slice(cols.start + c * MXU_COLS, cols.start + (c + 1) * MXU_COLS)
            acc = jnp.dot(xn, w_ref[:, blk], preferred_element_type=F32)
            o_ref[:, blk] = epilogue(acc, blk).astype(o_ref.dtype)
            _run_casts(cast_src, cast_dst, c, n_blocks)

    def rotary_section(cols, section):
        cos = cos_ref[...]
        sin = sin_ref[...]

        def epilogue(acc, blk):
            rot = _rotary(acc, cos, sin).reshape(tm // RET_CHUNK, RET_CHUNK, MXU_COLS)
            return (rot * decay_ref[section, :, blk]).reshape(tm, MXU_COLS)

        blockwise(cols, epilogue)

    def plain_section(cols):
        blockwise(cols, lambda acc, blk: acc)

    def silu_section(cols):
        blockwise(cols, lambda acc, blk: _silu(acc))

    def sigmoid_section(cols):
        blockwise(cols, lambda acc, blk: _sigmoid(acc))

    def pool_section(cols):
        seq_tile = i % tiles_per_seq

        @pl.when(seq_tile == 0)
        def _():
            halo_ref[...] = jnp.zeros_like(halo_ref)

        row = lax.broadcasted_iota(jnp.int32, (tm, 1), 0) + seq_tile * tm
        pos1 = (row + 1).astype(F32)
        xn = xn_ref[...]
        group = lambda gi: slice(gi * POOL_GROUP_DIM, (gi + 1) * POOL_GROUP_DIM)
        shifted = lambda gi: slice(cols.start + gi * POOL_GROUP_DIM,
                                   cols.start + (gi + 1) * POOL_GROUP_DIM)

        def pool(gi, u):
            s = jnp.concatenate([halo_ref[:, group(gi)], u], axis=0)
            halo_ref[:, group(gi)] = u[tm - POOL_HALO:, :]
            shift = 1
            while shift < POOL_WINDOWS[gi]:
                s = s + pltpu.roll(s, shift, axis=0)
                shift *= 2
            count = jnp.minimum(pos1, float(POOL_WINDOWS[gi]))
            return (s[POOL_HALO:, :] / count - u).astype(BF16)

        order = sorted(range(POOL_GROUPS), key=lambda gi: -POOL_WINDOWS[gi])
        pooled = {}
        for n, gi in enumerate(order):
            pooled[gi] = pool(gi, jnp.dot(xn, w_ref[:, shifted(gi)], preferred_element_type=F32))
            _run_casts(cast_src, cast_dst, n, POOL_GROUPS)
        for gi in order:
            mixed = jnp.dot(pooled[gi], pmix_ref[gi], preferred_element_type=F32)
            o_ref[:, shifted(gi)] = (mixed * pscale_ref[:, group(gi)]).astype(o_ref.dtype)

    def section_fn(index):
        off = index * SECTION
        if off < V_OFF:
            return rotary_section
        if off < G_OFF:
            return plain_section
        if off < U_OFF:
            return silu_section
        if off == U_OFF:
            return pool_section
        return sigmoid_section

    n_steps = pl.cdiv(N_SECTIONS, SECTIONS_PER_STEP)
    step_fns = [tuple(section_fn(s * SECTIONS_PER_STEP + h) for h in range(SECTIONS_PER_STEP)
                      if s * SECTIONS_PER_STEP + h < N_SECTIONS) for s in range(n_steps)]
    lo = 0
    while lo < n_steps:
        hi = lo + 1
        while hi < n_steps and step_fns[hi] == step_fns[lo]:
            hi += 1

        @pl.when((j >= lo) & (j < hi))
        def _(fns=step_fns[lo]):
            for half, fn in enumerate(fns):
                cols = slice(half * SECTION, (half + 1) * SECTION)
                if fn is rotary_section:
                    fn(cols, j * SECTIONS_PER_STEP + half)
                else:
                    fn(cols)

        lo = hi


def _inproj(xn, w_in_b, cos, sin, qk_decay, pool_mix, pool_scale, layer, seq, casts_of):
    t = xn.shape[0]
    tm, tn = 1024, SECTION * SECTIONS_PER_STEP
    assert seq % tm == 0
    grid = (t // tm, pl.cdiv(IN_WIDTH, tn))
    casts = casts_of(grid)
    kern = functools.partial(_inproj_kernel, tiles_per_seq=seq // tm, n_cast_src=len(casts),
                             n_cast_dst=len(_cast_out_specs(casts)))
    return pl.pallas_call(
        kern,
        out_shape=[jax.ShapeDtypeStruct((t, IN_WIDTH), BF16)] + _cast_out_shapes(casts),
        grid=grid,
        in_specs=[
            pl.BlockSpec((tm, D_MODEL), lambda i, j: (i, 0)),
            pl.BlockSpec((D_MODEL, tn), lambda i, j: (0, j)),
            pl.BlockSpec((tm, RET_QK_DIM), lambda i, j: (i, 0)),
            pl.BlockSpec((tm, RET_QK_DIM), lambda i, j: (i, 0)),
            pl.BlockSpec(qk_decay.shape, lambda i, j: (0, 0, 0)),
            pl.BlockSpec((None, POOL_GROUPS, POOL_GROUP_DIM, POOL_GROUP_DIM),
                         lambda i, j: (layer, 0, 0, 0)),
            pl.BlockSpec((None, 1, POOL_WIDTH), lambda i, j: (layer, 0, 0)),
        ] + [c.in_spec for c in casts],
        out_specs=[pl.BlockSpec((tm, tn), lambda i, j: (i, j))] + _cast_out_specs(casts),
        scratch_shapes=[pltpu.VMEM((POOL_HALO, POOL_WIDTH), F32)],
        compiler_params=_params(2),
        name="in_proj",
    )(xn, w_in_b, cos, sin, qk_decay, pool_mix, pool_scale, *[c.stacked for c in casts])


def _retention_tables():
    c = RET_CHUNK
    log_gamma = jnp.log(1.0 - 2.0 ** (-5.0 - jnp.arange(RET_HEADS, dtype=F32)))
    idx = jnp.arange(c, dtype=F32)
    diff = idx[:, None] - idx[None, :]
    causal = jnp.where(diff[None] >= 0, jnp.exp(-c * log_gamma)[:, None, None], 0.0)
    xi = jnp.exp((idx + 1.0)[None, :] * log_gamma[:, None])
    zeta = jnp.exp((c - 1.0 - idx)[None, :] * log_gamma[:, None])
    chunk_decay = jnp.exp(c * log_gamma)
    per_head = lambda a: jnp.broadcast_to(a.T[:, :, None], (c, RET_HEADS, RET_QK_DIM)).reshape(
        c, RET_QK_WIDTH)
    qk_decay = jnp.stack([per_head(xi) * (RET_QK_DIM ** -0.5), per_head(zeta)])
    chunk_decay = jnp.broadcast_to(chunk_decay[:, None, None], (RET_HEADS, 1, RET_V_DIM))
    return qk_decay, causal, chunk_decay


GATE_BLOCK = 1024
assert GATE_OFF % GATE_BLOCK == 0 and D_MODEL % GATE_BLOCK == 0 and GATE_BLOCK % MXU_COLS == 0


def _mixer_kernel(*refs):
    qx_ref, kz_ref, v_ref, g_ref, yp_ref = refs[:5]
    n_gate = D_MODEL // GATE_BLOCK
    g0_refs = refs[5:5 + n_gate]
    g1_refs = refs[5 + n_gate:5 + 2 * n_gate]
    (causal_ref, cdecay_ref, wr_ref, wp_ref, o_ref,
     state_ref, yret_ref, pool_ref) = refs[5 + 2 * n_gate:]
    c = RET_CHUNK

    @pl.when(pl.program_id(1) == 0)
    def _():
        state_ref[...] = jnp.zeros_like(state_ref)

    def gate(gate_refs, cols):
        ref = gate_refs[cols.start // GATE_BLOCK]
        lo = cols.start % GATE_BLOCK
        return ref[:, lo:lo + MXU_COLS].astype(F32)

    out_blocks = _col_blocks(D_MODEL)
    n_chunks = qx_ref.shape[0] // c
    pool_blocks_per_chunk = pl.cdiv(len(out_blocks), n_chunks)
    yp = yp_ref[...]
    pending = list(out_blocks)

    def pool_branch(n_blocks):
        for _ in range(min(n_blocks, len(pending))):
            cols = pending.pop(0)
            p = jnp.dot(yp, wp_ref[:, cols], preferred_element_type=F32)
            pool_ref[:, cols] = gate(g1_refs, cols) * p

    heads = range(RET_HEADS)
    qk_cols = [slice(h * RET_QK_DIM, (h + 1) * RET_QK_DIM) for h in heads]
    v_cols = [slice(h * RET_V_DIM, (h + 1) * RET_V_DIM) for h in heads]
    for n in range(n_chunks):
        rows = slice(n * c, (n + 1) * c)
        qx = [qx_ref[rows, qk_cols[h]] for h in heads]
        kz = [kz_ref[rows, qk_cols[h]] for h in heads]
        v = [v_ref[rows, v_cols[h]] for h in heads]
        scores = [lax.dot_general(qx[h], kz[h], (((1,), (1,)), ((), ())),
                                  preferred_element_type=F32) for h in heads]
        kv = [lax.dot_general(kz[h], v[h], (((0,), (0,)), ((), ())),
                              preferred_element_type=F32) for h in heads]
        pool_branch(pool_blocks_per_chunk)
        state = [state_ref[h] for h in heads]
        lhs = [jnp.concatenate([(scores[h] * causal_ref[h]).astype(BF16), qx[h]], axis=1)
               for h in heads]
        rhs = [jnp.concatenate([v[h], state[h].astype(BF16)], axis=0) for h in heads]
        y = [jnp.dot(lhs[h], rhs[h], preferred_element_type=F32) for h in heads]
        for h in heads:
            state_ref[h] = state[h] * cdecay_ref[h] + kv[h]
            yn = y[h] * lax.rsqrt(jnp.mean(y[h] * y[h], axis=-1, keepdims=True) + NORM_EPS)
            yret_ref[rows, v_cols[h]] = yn.astype(yret_ref.dtype) * g_ref[rows, v_cols[h]]
    pool_branch(len(pending))

    yret = yret_ref[...]
    for cols in out_blocks:
        r = jnp.dot(yret, wr_ref[:, cols], preferred_element_type=F32)
        o_ref[:, cols] = (gate(g0_refs, cols) * r + pool_ref[:, cols]).astype(o_ref.dtype)


def _mixer(proj, causal, cdecay, w_ret_up_b, w_pool_up_b, batch, seq):
    t = proj.shape[0]
    ts = 512
    spb = seq // ts
    tok_blk = lambda width, off: pl.BlockSpec((ts, width), lambda b, s: (b * spb + s, off // width))
    full_blk = lambda a: pl.BlockSpec(a.shape, lambda b, s: (0,) * a.ndim)
    weight_blk = lambda a: pl.BlockSpec(a.shape, lambda b, s: (0, 0), pipeline_mode=pl.Buffered(1))
    gate_blks = [tok_blk(GATE_BLOCK, GATE_OFF + branch * D_MODEL + part * GATE_BLOCK)
                 for branch in range(N_BRANCHES) for part in range(D_MODEL // GATE_BLOCK)]
    return pl.pallas_call(
        _mixer_kernel,
        out_shape=jax.ShapeDtypeStruct((t, D_MODEL), BF16),
        grid=(batch, spb),
        in_specs=[
            tok_blk(RET_QK_WIDTH, Q_OFF), tok_blk(RET_QK_WIDTH, K_OFF),
            tok_blk(RET_V_WIDTH, V_OFF), tok_blk(RET_V_WIDTH, G_OFF),
            tok_blk(POOL_WIDTH, U_OFF),
            *gate_blks,
            full_blk(causal), full_blk(cdecay),
            weight_blk(w_ret_up_b), weight_blk(w_pool_up_b),
        ],
        out_specs=tok_blk(D_MODEL, 0),
        scratch_shapes=[
            pltpu.VMEM((RET_HEADS, RET_QK_DIM, RET_V_DIM), F32),
            pltpu.VMEM((ts, RET_V_WIDTH), BF16),
            pltpu.VMEM((ts, D_MODEL), F32),
        ],
        compiler_params=_params(2),
        name="mixer",
    )(proj, proj, proj, proj, proj, *([proj] * len(gate_blks)),
      causal, cdecay, w_ret_up_b, w_pool_up_b)


def _residual_norm_kernel(*refs, emit_residual):
    x_ref, a_ref, w_ref, gain_ref = refs[:4]
    if emit_residual:
        res_ref, xn_ref = refs[4:]
    else:
        (xn_ref,) = refs[4:]
        res_ref = xn_ref
    a = a_ref[...]
    for c in range(res_ref.shape[1] // MXU_COLS):
        cols = slice(c * MXU_COLS, (c + 1) * MXU_COLS)
        res_ref[:, cols] = x_ref[:, cols] + jnp.dot(a, w_ref[:, cols], preferred_element_type=F32)
    _rmsnorm_rows(res_ref, gain_ref, xn_ref)


def _residual_norm(x, a, w_b, gain, tm, name, xn_dtype, emit_residual=True):
    t, k = a.shape
    row_blk = lambda: pl.BlockSpec((tm, D_MODEL), lambda i: (i, 0))
    out_shape = [jax.ShapeDtypeStruct(x.shape, xn_dtype)]
    out_specs = [row_blk()]
    if emit_residual:
        out_shape.insert(0, jax.ShapeDtypeStruct(x.shape, x.dtype))
        out_specs.insert(0, row_blk())
    return pl.pallas_call(
        functools.partial(_residual_norm_kernel, emit_residual=emit_residual),
        out_shape=out_shape,
        grid=(t // tm,),
        in_specs=[
            row_blk(),
            pl.BlockSpec((tm, k), lambda i: (i, 0)),
            pl.BlockSpec((k, D_MODEL), lambda i: (0, 0), pipeline_mode=pl.Buffered(1)),
            pl.BlockSpec((1, D_MODEL), lambda i: (0, 0)),
        ],
        out_specs=out_specs,
        compiler_params=_params(1),
        name=name,
    )(x, a, w_b, gain)


def _ffn_up_kernel(*refs, d_ff, n_cast_src, n_cast_dst):
    xn_ref, wg_ref, wu_ref = refs[:3]
    cast_src = refs[3:3 + n_cast_src]
    o_ref = refs[3 + n_cast_src]
    cast_dst = refs[4 + n_cast_src:4 + n_cast_src + n_cast_dst]
    j = pl.program_id(1)
    tn = o_ref.shape[1]

    def body(n_blocks):
        xn = xn_ref[...]
        for c in range(n_blocks):
            cols = slice(c * MXU_COLS, (c + 1) * MXU_COLS)
            gate = jnp.dot(xn, wg_ref[:, cols], preferred_element_type=F32)
            up = jnp.dot(xn, wu_ref[:, cols], preferred_element_type=F32)
            o_ref[:, cols] = (_silu(gate) * up).astype(o_ref.dtype)
            _run_casts(cast_src, cast_dst, c, n_blocks)

    full_steps, tail = divmod(d_ff, tn)

    @pl.when(j < full_steps)
    def _():
        body(tn // MXU_COLS)

    if tail:
        @pl.when(j == full_steps)
        def _():
            body(tail // MXU_COLS)


def _ffn_up(xn, w_g_b, w_u_b, casts_of):
    t = xn.shape[0]
    d_ff = w_g_b.shape[1]
    tm, tn = 1024, 1024
    assert d_ff % MXU_COLS == 0
    grid = (t // tm, pl.cdiv(d_ff, tn))
    casts = casts_of(grid)
    kern = functools.partial(_ffn_up_kernel, d_ff=d_ff, n_cast_src=len(casts),
                             n_cast_dst=len(_cast_out_specs(casts)))
    return pl.pallas_call(
        kern,
        out_shape=[jax.ShapeDtypeStruct((t, d_ff), BF16)] + _cast_out_shapes(casts),
        grid=grid,
        in_specs=[
            pl.BlockSpec((tm, D_MODEL), lambda i, j: (i, 0)),
            pl.BlockSpec((D_MODEL, tn), lambda i, j: (0, j)),
            pl.BlockSpec((D_MODEL, tn), lambda i, j: (0, j)),
        ] + [c.in_spec for c in casts],
        out_specs=[pl.BlockSpec((tm, tn), lambda i, j: (i, j))] + _cast_out_specs(casts),
        compiler_params=_params(2),
        name="ffn_up",
    )(xn, w_g_b, w_u_b, *[c.stacked for c in casts])


def kernel(x, positions, norm1, w_in, pool_mix, pool_scale, w_ret_up, w_pool_up, w_o,
           norm2, w_gu, w_down, norm_f):
    batch, seq, _ = x.shape
    depth = w_in.shape[0]
    t = batch * seq
    xt = x.reshape(t, D_MODEL)

    pool_mix_b = pool_mix.astype(BF16)
    pool_scale_r = pool_scale.reshape(depth, 1, POOL_WIDTH)

    qk_decay, causal, chunk_decay = _retention_tables()

    w_in_b = w_in[0].astype(BF16)
    xn, cos, sin = _input_stage(xt, norm1[0][None, :], positions)
    mixer = (w_ret_up, w_pool_up, w_o)
    for layer in range(depth):
        last = layer + 1 == depth
        proj, w_g_b, w_u_b, w_down_b, *mixer_b = _inproj(
            xn, w_in_b, cos, sin, qk_decay, pool_mix_b, pool_scale_r, layer, seq,
            lambda grid: [_cast_stream(w_gu, layer, grid, n_split=2),
                          _cast_stream(w_down, layer, grid)]
            + [_cast_stream(w, layer, grid) for w in (mixer if layer == 0 else ())])
        if mixer_b:
            w_ret_up_b, w_pool_up_b, w_o_b = mixer_b
        merged = _mixer(proj, causal, chunk_decay, w_ret_up_b, w_pool_up_b, batch, seq)
        xt, xn = _residual_norm(xt, merged, w_o_b, norm2[layer][None, :], 512, "out_proj", BF16)
        hidden, *next_b = _ffn_up(
            xn, w_g_b, w_u_b,
            lambda grid: [] if last else [
                _cast_stream(w, layer + 1, grid) for w in (w_in,) + mixer])
        if last:
            (out,) = _residual_norm(xt, hidden, w_down_b, norm_f[None, :], 512, "ffn_down", F32,
                                    emit_residual=False)
        else:
            w_in_b, w_ret_up_b, w_pool_up_b, w_o_b = next_b
            xt, xn = _residual_norm(xt, hidden, w_down_b, norm1[layer + 1][None, :], 512,
                                    "ffn_down", BF16)
    return out.reshape(batch, seq, D_MODEL)
```

```python
import functools
from typing import NamedTuple

import jax
import jax.numpy as jnp
from jax import lax
from jax.experimental import pallas as pl
from jax.experimental.pallas import tpu as pltpu

D_MODEL = 2048
RET_HEADS = 8
RET_QK_DIM = 128
RET_V_DIM = 256
RET_QK_WIDTH = RET_HEADS * RET_QK_DIM
RET_V_WIDTH = RET_HEADS * RET_V_DIM
RET_CHUNK = 128
ROPE_BASE = 10000.0
POOL_WINDOWS = (2, 4, 8, 16)
POOL_GROUPS = len(POOL_WINDOWS)
POOL_GROUP_DIM = D_MODEL // 8
POOL_WIDTH = POOL_GROUPS * POOL_GROUP_DIM
N_BRANCHES = 2
IN_WIDTH = 2 * RET_QK_WIDTH + 2 * RET_V_WIDTH + POOL_WIDTH + N_BRANCHES * D_MODEL
NORM_EPS = 1e-6

Q_OFF = 0
K_OFF = RET_QK_WIDTH
V_OFF = 2 * RET_QK_WIDTH
G_OFF = V_OFF + RET_V_WIDTH
U_OFF = G_OFF + RET_V_WIDTH
GATE_OFF = U_OFF + POOL_WIDTH

POOL_HALO = 16
assert POOL_HALO >= max(POOL_WINDOWS) and POOL_HALO % 8 == 0

VMEM_LIMIT_BYTES = 56 * 1024 * 1024
MXU_COLS = 256
assert POOL_GROUP_DIM == MXU_COLS

TOKEN_TILE = 1024
ROW_TILE = 512
FFN_TILE_COLS = 1024

BF16 = jnp.bfloat16
F32 = jnp.float32


def _params(n_axes):
    return pltpu.CompilerParams(
        dimension_semantics=("arbitrary",) * n_axes,
        vmem_limit_bytes=VMEM_LIMIT_BYTES,
    )


BF16_SUBLANES = 16


class _CastStream(NamedTuple):
    stacked: jax.Array
    in_spec: pl.BlockSpec
    out_specs: tuple
    out_shapes: tuple


def _cast_stream(stacked, layer, grid, n_split=1):
    _, k, n = stacked.shape
    n_steps = grid[0] * grid[1]
    rows = BF16_SUBLANES
    while k % rows or k // rows > n_steps:
        rows += BF16_SUBLANES
    last = k // rows - 1
    block = lambda i, j: jnp.minimum(i * grid[1] + j, last)
    return _CastStream(
        stacked,
        pl.BlockSpec((None, rows, n), lambda i, j: (layer, block(i, j), 0)),
        (pl.BlockSpec((rows, n // n_split), lambda i, j: (block(i, j), 0)),) * n_split,
        (jax.ShapeDtypeStruct((k, n // n_split), BF16),) * n_split,
    )


def _cast_out_specs(casts):
    return [s for c in casts for s in c.out_specs]


def _cast_out_shapes(casts):
    return [s for c in casts for s in c.out_shapes]


def _run_casts(src_refs, dst_refs, part=0, n_parts=1):
    d = 0
    for src in src_refs:
        col = 0
        while col < src.shape[1]:
            dst = dst_refs[d]
            width = dst.shape[1] // n_parts
            lo = part * width
            dst[:, lo:lo + width] = src[:, col + lo:col + lo + width].astype(dst.dtype)
            col += dst.shape[1]
            d += 1
    assert d == len(dst_refs)


def _sigmoid(x):
    return 0.5 * jnp.tanh(0.5 * x) + 0.5


def _silu(x):
    h = 0.5 * x
    return h + h * jnp.tanh(h)


def _rotary(t, cos, sin):
    outs = []
    for h in range(t.shape[1] // RET_QK_DIM):
        th = t[:, h * RET_QK_DIM:(h + 1) * RET_QK_DIM]
        outs.append(th * cos + pltpu.roll(th, RET_QK_DIM // 2, axis=1) * sin)
    return jnp.concatenate(outs, axis=1)


def _col_blocks(width):
    return [slice(c, c + MXU_COLS) for c in range(0, width, MXU_COLS)]


def _rmsnorm_rows(x_ref, gain_ref, out_ref):
    n_blocks = x_ref.shape[1] // MXU_COLS
    col = lambda c: slice(c * MXU_COLS, (c + 1) * MXU_COLS)
    sq = None
    for c in range(n_blocks):
        xc = x_ref[:, col(c)]
        sq = xc * xc if sq is None else sq + xc * xc
    scale = lax.rsqrt(jnp.sum(sq, axis=-1, keepdims=True) / x_ref.shape[1] + NORM_EPS)
    for c in range(n_blocks):
        out_ref[:, col(c)] = (x_ref[:, col(c)] * scale * gain_ref[:, col(c)]).astype(out_ref.dtype)


def _input_kernel(x_ref, gain_ref, pos_ref, freq_ref, xn_ref, cos_ref, sin_ref):
    _rmsnorm_rows(x_ref, gain_ref, xn_ref)
    half_rows = pos_ref.shape[0]
    ang = pos_ref[...] * freq_ref[...]
    lane = lax.broadcasted_iota(jnp.int32, ang.shape, 1)
    low = lane < RET_QK_DIM // 2
    cos = jnp.cos(ang)
    sin = jnp.sin(ang)
    cos_swapped = pltpu.roll(cos, RET_QK_DIM // 2, axis=1)
    sin_swapped = pltpu.roll(sin, RET_QK_DIM // 2, axis=1)
    cos_ref[:half_rows, :] = jnp.where(low, cos, cos_swapped)
    sin_ref[:half_rows, :] = jnp.where(low, -sin, sin_swapped)
    cos_ref[half_rows:, :] = jnp.where(low, cos_swapped, cos)
    sin_ref[half_rows:, :] = jnp.where(low, -sin_swapped, sin)


def _input_stage(x, gain, positions):
    t = x.shape[0]
    tm = TOKEN_TILE
    half = RET_QK_DIM // 2
    inv_freq = ROPE_BASE ** (-jnp.arange(half, dtype=F32) / half)
    freq = jnp.concatenate([inv_freq, inv_freq])[None, :]
    pos = positions.reshape(t // tm, 2, tm // 2).astype(F32).transpose(0, 2, 1)
    pos = jnp.broadcast_to(pos[..., None], (t // tm, tm // 2, 2, half)).reshape(t // 2, RET_QK_DIM)
    rows = lambda n, width: pl.BlockSpec((n, width), lambda i: (i, 0))
    whole = lambda width: pl.BlockSpec((1, width), lambda i: (0, 0))
    return pl.pallas_call(
        _input_kernel,
        out_shape=[jax.ShapeDtypeStruct(x.shape, BF16),
                   jax.ShapeDtypeStruct((t, RET_QK_DIM), F32),
                   jax.ShapeDtypeStruct((t, RET_QK_DIM), F32)],
        grid=(t // tm,),
        in_specs=[rows(tm, D_MODEL), whole(D_MODEL), rows(tm // 2, RET_QK_DIM), whole(RET_QK_DIM)],
        out_specs=[rows(tm, D_MODEL), rows(tm, RET_QK_DIM), rows(tm, RET_QK_DIM)],
        compiler_params=_params(1),
        name="input_stage",
    )(x, gain, pos, freq)


SECTION = 1024
SECTIONS_PER_STEP = 1
N_SECTIONS = IN_WIDTH // SECTION
assert all(off % SECTION == 0 for off in (K_OFF, V_OFF, G_OFF, U_OFF, GATE_OFF))
assert RET_QK_WIDTH == SECTION and POOL_WIDTH == SECTION


def _inproj_kernel(*refs, tiles_per_seq, n_cast_src, n_cast_dst):
    xn_ref, w_ref, cos_ref, sin_ref, decay_ref, pmix_ref, pscale_ref = refs[:7]
    cast_src = refs[7:7 + n_cast_src]
    o_ref = refs[7 + n_cast_src]
    cast_dst = refs[8 + n_cast_src:8 + n_cast_src + n_cast_dst]
    halo_ref = refs[8 + n_cast_src + n_cast_dst]
    i = pl.program_id(0)
    j = pl.program_id(1)
    tm = xn_ref.shape[0]

    def blockwise(cols, epilogue):
        n_blocks = (cols.stop - cols.start) // MXU_COLS
        xn = xn_ref[...]
        for c in range(n_blocks):
            blk = slice(cols.start + c * MXU_COLS, cols.start + (c + 1) * MXU_COLS)
            acc = jnp.dot(xn, w_ref[:, blk], preferred_element_type=F32)
            o_ref[:, blk] = epilogue(acc, blk).astype(o_ref.dtype)
            _run_casts(cast_src, cast_dst, c, n_blocks)

    def rotary_section(cols, section):
        cos = cos_ref[...]
        sin = sin_ref[...]

        def epilogue(acc, blk):
            rot = _rotary(acc, cos, sin).reshape(tm // RET_CHUNK, RET_CHUNK, MXU_COLS)
            return (rot * decay_ref[section, :, blk]).reshape(tm, MXU_COLS)

        blockwise(cols, epilogue)

    def plain_section(cols):
        blockwise(cols, lambda acc, blk: acc)

    def silu_section(cols):
        blockwise(cols, lambda acc, blk: _silu(acc))

    def sigmoid_section(cols):
        blockwise(cols, lambda acc, blk: _sigmoid(acc))

    def pool_section(cols):
        seq_tile = i % tiles_per_seq

        @pl.when(seq_tile == 0)
        def _():
            halo_ref[...] = jnp.zeros_like(halo_ref)

        row = lax.broadcasted_iota(jnp.int32, (tm, 1), 0) + seq_tile * tm
        pos1 = (row + 1).astype(F32)
        xn = xn_ref[...]
        group = lambda gi: slice(gi * POOL_GROUP_DIM, (gi + 1) * POOL_GROUP_DIM)
        shifted = lambda gi: slice(cols.start + gi * POOL_GROUP_DIM,
                                   cols.start + (gi + 1) * POOL_GROUP_DIM)

        def pool(gi, u):
            s = jnp.concatenate([halo_ref[:, group(gi)], u], axis=0)
            halo_ref[:, group(gi)] = u[tm - POOL_HALO:, :]
            shift = 1
            while shift < POOL_WINDOWS[gi]:
                s = s + pltpu.roll(s, shift, axis=0)
                shift *= 2
            count = jnp.minimum(pos1, float(POOL_WINDOWS[gi]))
            return (s[POOL_HALO:, :] / count - u).astype(BF16)

        order = sorted(range(POOL_GROUPS), key=lambda gi: -POOL_WINDOWS[gi])
        pooled = {}
        for n, gi in enumerate(order):
            pooled[gi] = pool(gi, jnp.dot(xn, w_ref[:, shifted(gi)], preferred_element_type=F32))
            _run_casts(cast_src, cast_dst, n, POOL_GROUPS)
        for gi in order:
            mixed = jnp.dot(pooled[gi], pmix_ref[gi], preferred_element_type=F32)
            o_ref[:, shifted(gi)] = (mixed * pscale_ref[:, group(gi)]).astype(o_ref.dtype)

    def section_fn(index):
        off = index * SECTION
        if off < V_OFF:
            return rotary_section
        if off < G_OFF:
            return plain_section
        if off < U_OFF:
            return silu_section
        if off == U_OFF:
            return pool_section
        return sigmoid_section

    n_steps = pl.cdiv(N_SECTIONS, SECTIONS_PER_STEP)
    step_fns = [tuple(section_fn(s * SECTIONS_PER_STEP + h) for h in range(SECTIONS_PER_STEP)
                      if s * SECTIONS_PER_STEP + h < N_SECTIONS) for s in range(n_steps)]
    lo = 0
    while lo < n_steps:
        hi = lo + 1
        while hi < n_steps and step_fns[hi] == step_fns[lo]:
            hi += 1

        @pl.when((j >= lo) & (j < hi))
        def _(fns=step_fns[lo]):
            for half, fn in enumerate(fns):
                cols = slice(half * SECTION, (half + 1) * SECTION)
                if fn is rotary_section:
                    fn(cols, j * SECTIONS_PER_STEP + half)
                else:
                    fn(cols)

        lo = hi


def _inproj(xn, w_in_b, cos, sin, qk_decay, pool_mix, pool_scale, layer, seq, casts_of):
    t = xn.shape[0]
    tm, tn = TOKEN_TILE, SECTION * SECTIONS_PER_STEP
    assert seq % tm == 0
    grid = (t // tm, pl.cdiv(IN_WIDTH, tn))
    casts = casts_of(grid)
    kern = functools.partial(_inproj_kernel, tiles_per_seq=seq // tm, n_cast_src=len(casts),
                             n_cast_dst=len(_cast_out_specs(casts)))
    return pl.pallas_call(
        kern,
        out_shape=[jax.ShapeDtypeStruct((t, IN_WIDTH), BF16)] + _cast_out_shapes(casts),
        grid=grid,
        in_specs=[
            pl.BlockSpec((tm, D_MODEL), lambda i, j: (i, 0)),
            pl.BlockSpec((D_MODEL, tn), lambda i, j: (0, j)),
            pl.BlockSpec((tm, RET_QK_DIM), lambda i, j: (i, 0)),
            pl.BlockSpec((tm, RET_QK_DIM), lambda i, j: (i, 0)),
            pl.BlockSpec(qk_decay.shape, lambda i, j: (0, 0, 0)),
            pl.BlockSpec((None, POOL_GROUPS, POOL_GROUP_DIM, POOL_GROUP_DIM),
                         lambda i, j: (layer, 0, 0, 0)),
            pl.BlockSpec((None, 1, POOL_WIDTH), lambda i, j: (layer, 0, 0)),
        ] + [c.in_spec for c in casts],
        out_specs=[pl.BlockSpec((tm, tn), lambda i, j: (i, j))] + _cast_out_specs(casts),
        scratch_shapes=[pltpu.VMEM((POOL_HALO, POOL_WIDTH), F32)],
        compiler_params=_params(2),
        name="in_proj",
    )(xn, w_in_b, cos, sin, qk_decay, pool_mix, pool_scale, *[c.stacked for c in casts])


def _retention_tables():
    c = RET_CHUNK
    log_gamma = jnp.log(1.0 - 2.0 ** (-5.0 - jnp.arange(RET_HEADS, dtype=F32)))
    idx = jnp.arange(c, dtype=F32)
    diff = idx[:, None] - idx[None, :]
    causal = jnp.where(diff[None] >= 0, jnp.exp(-c * log_gamma)[:, None, None], 0.0)
    xi = jnp.exp((idx + 1.0)[None, :] * log_gamma[:, None])
    zeta = jnp.exp((c - 1.0 - idx)[None, :] * log_gamma[:, None])
    chunk_decay = jnp.exp(c * log_gamma)
    per_head = lambda a: jnp.broadcast_to(a.T[:, :, None], (c, RET_HEADS, RET_QK_DIM)).reshape(
        c, RET_QK_WIDTH)
    qk_decay = jnp.stack([per_head(xi) * (RET_QK_DIM ** -0.5), per_head(zeta)])
    chunk_decay = jnp.broadcast_to(chunk_decay[:, None, None], (RET_HEADS, 1, RET_V_DIM))
    return qk_decay, causal, chunk_decay


GATE_BLOCK = 1024
assert GATE_OFF % GATE_BLOCK == 0 and D_MODEL % GATE_BLOCK == 0 and GATE_BLOCK % MXU_COLS == 0


def _mixer_kernel(*refs):
    qx_ref, kz_ref, v_ref, g_ref, yp_ref = refs[:5]
    n_gate = D_MODEL // GATE_BLOCK
    g0_refs = refs[5:5 + n_gate]
    g1_refs = refs[5 + n_gate:5 + 2 * n_gate]
    (causal_ref, cdecay_ref, wr_ref, wp_ref, o_ref,
     state_ref, yret_ref, pool_ref) = refs[5 + 2 * n_gate:]
    c = RET_CHUNK

    @pl.when(pl.program_id(1) == 0)
    def _():
        state_ref[...] = jnp.zeros_like(state_ref)

    def gate(gate_refs, cols):
        ref = gate_refs[cols.start // GATE_BLOCK]
        lo = cols.start % GATE_BLOCK
        return ref[:, lo:lo + MXU_COLS].astype(F32)

    out_blocks = _col_blocks(D_MODEL)
    n_chunks = qx_ref.shape[0] // c
    pool_blocks_per_chunk = pl.cdiv(len(out_blocks), n_chunks)
    yp = yp_ref[...]
    pending = list(out_blocks)

    def pool_branch(n_blocks):
        for _ in range(min(n_blocks, len(pending))):
            cols = pending.pop(0)
            p = jnp.dot(yp, wp_ref[:, cols], preferred_element_type=F32)
            pool_ref[:, cols] = gate(g1_refs, cols) * p

    heads = range(RET_HEADS)
    qk_cols = [slice(h * RET_QK_DIM, (h + 1) * RET_QK_DIM) for h in heads]
    v_cols = [slice(h * RET_V_DIM, (h + 1) * RET_V_DIM) for h in heads]
    for n in range(n_chunks):
        rows = slice(n * c, (n + 1) * c)
        qx = [qx_ref[rows, qk_cols[h]] for h in heads]
        kz = [kz_ref[rows, qk_cols[h]] for h in heads]
        v = [v_ref[rows, v_cols[h]] for h in heads]
        scores = [lax.dot_general(qx[h], kz[h], (((1,), (1,)), ((), ())),
                                  preferred_element_type=F32) for h in heads]
        kv = [lax.dot_general(kz[h], v[h], (((0,), (0,)), ((), ())),
                              preferred_element_type=F32) for h in heads]
        pool_branch(pool_blocks_per_chunk)
        state = [state_ref[h] for h in heads]
        lhs = [jnp.concatenate([(scores[h] * causal_ref[h]).astype(BF16), qx[h]], axis=1)
               for h in heads]
        rhs = [jnp.concatenate([v[h], state[h].astype(BF16)], axis=0) for h in heads]
        y = [jnp.dot(lhs[h], rhs[h], preferred_element_type=F32) for h in heads]
        for h in heads:
            state_ref[h] = state[h] * cdecay_ref[h] + kv[h]
            yn = y[h] * lax.rsqrt(jnp.mean(y[h] * y[h], axis=-1, keepdims=True) + NORM_EPS)
            yret_ref[rows, v_cols[h]] = yn.astype(yret_ref.dtype) * g_ref[rows, v_cols[h]]
    pool_branch(len(pending))

    yret = yret_ref[...]
    for cols in out_blocks:
        r = jnp.dot(yret, wr_ref[:, cols], preferred_element_type=F32)
        o_ref[:, cols] = (gate(g0_refs, cols) * r + pool_ref[:, cols]).astype(o_ref.dtype)


def _mixer(proj, causal, cdecay, w_ret_up_b, w_pool_up_b, batch, seq):
    t = proj.shape[0]
    ts = ROW_TILE
    spb = seq // ts
    tok_blk = lambda width, off: pl.BlockSpec((ts, width), lambda b, s: (b * spb + s, off // width))
    full_blk = lambda a: pl.BlockSpec(a.shape, lambda b, s: (0,) * a.ndim)
    weight_blk = lambda a: pl.BlockSpec(a.shape, lambda b, s: (0, 0), pipeline_mode=pl.Buffered(1))
    gate_blks = [tok_blk(GATE_BLOCK, GATE_OFF + branch * D_MODEL + part * GATE_BLOCK)
                 for branch in range(N_BRANCHES) for part in range(D_MODEL // GATE_BLOCK)]
    return pl.pallas_call(
        _mixer_kernel,
        out_shape=jax.ShapeDtypeStruct((t, D_MODEL), BF16),
        grid=(batch, spb),
        in_specs=[
            tok_blk(RET_QK_WIDTH, Q_OFF), tok_blk(RET_QK_WIDTH, K_OFF),
            tok_blk(RET_V_WIDTH, V_OFF), tok_blk(RET_V_WIDTH, G_OFF),
            tok_blk(POOL_WIDTH, U_OFF),
            *gate_blks,
            full_blk(causal), full_blk(cdecay),
            weight_blk(w_ret_up_b), weight_blk(w_pool_up_b),
        ],
        out_specs=tok_blk(D_MODEL, 0),
        scratch_shapes=[
            pltpu.VMEM((RET_HEADS, RET_QK_DIM, RET_V_DIM), F32),
            pltpu.VMEM((ts, RET_V_WIDTH), BF16),
            pltpu.VMEM((ts, D_MODEL), F32),
        ],
        compiler_params=_params(2),
        name="mixer",
    )(proj, proj, proj, proj, proj, *([proj] * len(gate_blks)),
      causal, cdecay, w_ret_up_b, w_pool_up_b)


def _residual_norm_kernel(*refs, emit_residual):
    x_ref, a_ref, w_ref, gain_ref = refs[:4]
    if emit_residual:
        res_ref, xn_ref = refs[4:]
    else:
        (xn_ref,) = refs[4:]
        res_ref = xn_ref
    a = a_ref[...]
    for c in range(res_ref.shape[1] // MXU_COLS):
        cols = slice(c * MXU_COLS, (c + 1) * MXU_COLS)
        res_ref[:, cols] = x_ref[:, cols] + jnp.dot(a, w_ref[:, cols], preferred_element_type=F32)
    _rmsnorm_rows(res_ref, gain_ref, xn_ref)


def _residual_norm(x, a, w_b, gain, name, xn_dtype, emit_residual=True):
    t, k = a.shape
    tm = ROW_TILE
    row_blk = lambda: pl.BlockSpec((tm, D_MODEL), lambda i: (i, 0))
    out_shape = [jax.ShapeDtypeStruct(x.shape, xn_dtype)]
    out_specs = [row_blk()]
    if emit_residual:
        out_shape.insert(0, jax.ShapeDtypeStruct(x.shape, x.dtype))
        out_specs.insert(0, row_blk())
    return pl.pallas_call(
        functools.partial(_residual_norm_kernel, emit_residual=emit_residual),
        out_shape=out_shape,
        grid=(t // tm,),
        in_specs=[
            row_blk(),
            pl.BlockSpec((tm, k), lambda i: (i, 0)),
            pl.BlockSpec((k, D_MODEL), lambda i: (0, 0), pipeline_mode=pl.Buffered(1)),
            pl.BlockSpec((1, D_MODEL), lambda i: (0, 0)),
        ],
        out_specs=out_specs,
        compiler_params=_params(1),
        name=name,
    )(x, a, w_b, gain)


def _ffn_up_kernel(*refs, d_ff, n_cast_src, n_cast_dst):
    xn_ref, wg_ref, wu_ref = refs[:3]
    cast_src = refs[3:3 + n_cast_src]
    o_ref = refs[3 + n_cast_src]
    cast_dst = refs[4 + n_cast_src:4 + n_cast_src + n_cast_dst]
    j = pl.program_id(1)
    tn = o_ref.shape[1]

    def body(n_blocks):
        xn = xn_ref[...]
        for c in range(n_blocks):
            cols = slice(c * MXU_COLS, (c + 1) * MXU_COLS)
            gate = jnp.dot(xn, wg_ref[:, cols], preferred_element_type=F32)
            up = jnp.dot(xn, wu_ref[:, cols], preferred_element_type=F32)
            o_ref[:, cols] = (_silu(gate) * up).astype(o_ref.dtype)
            _run_casts(cast_src, cast_dst, c, n_blocks)

    full_steps, tail = divmod(d_ff, tn)

    @pl.when(j < full_steps)
    def _():
        body(tn // MXU_COLS)

    if tail:
        @pl.when(j == full_steps)
        def _():
            body(tail // MXU_COLS)


def _ffn_up(xn, w_g_b, w_u_b, casts_of):
    t = xn.shape[0]
    d_ff = w_g_b.shape[1]
    tm, tn = TOKEN_TILE, FFN_TILE_COLS
    assert d_ff % MXU_COLS == 0
    grid = (t // tm, pl.cdiv(d_ff, tn))
    casts = casts_of(grid)
    kern = functools.partial(_ffn_up_kernel, d_ff=d_ff, n_cast_src=len(casts),
                             n_cast_dst=len(_cast_out_specs(casts)))
    return pl.pallas_call(
        kern,
        out_shape=[jax.ShapeDtypeStruct((t, d_ff), BF16)] + _cast_out_shapes(casts),
        grid=grid,
        in_specs=[
            pl.BlockSpec((tm, D_MODEL), lambda i, j: (i, 0)),
            pl.BlockSpec((D_MODEL, tn), lambda i, j: (0, j)),
            pl.BlockSpec((D_MODEL, tn), lambda i, j: (0, j)),
        ] + [c.in_spec for c in casts],
        out_specs=[pl.BlockSpec((tm, tn), lambda i, j: (i, j))] + _cast_out_specs(casts),
        compiler_params=_params(2),
        name="ffn_up",
    )(xn, w_g_b, w_u_b, *[c.stacked for c in casts])


def kernel(x, positions, norm1, w_in, pool_mix, pool_scale, w_ret_up, w_pool_up, w_o,
           norm2, w_gu, w_down, norm_f):
    batch, seq, _ = x.shape
    depth = w_in.shape[0]
    t = batch * seq
    xt = x.reshape(t, D_MODEL)

    pool_mix_b = pool_mix.astype(BF16)
    pool_scale_r = pool_scale.reshape(depth, 1, POOL_WIDTH)

    qk_decay, causal, chunk_decay = _retention_tables()

    w_in_b = w_in[0].astype(BF16)
    xn, cos, sin = _input_stage(xt, norm1[0][None, :], positions)
    mixer = (w_ret_up, w_pool_up, w_o)
    for layer in range(depth):
        last = layer + 1 == depth
        proj, w_g_b, w_u_b, w_down_b, *mixer_b = _inproj(
            xn, w_in_b, cos, sin, qk_decay, pool_mix_b, pool_scale_r, layer, seq,
            lambda grid: [_cast_stream(w_gu, layer, grid, n_split=2),
                          _cast_stream(w_down, layer, grid)]
            + [_cast_stream(w, layer, grid) for w in (mixer if layer == 0 else ())])
        if mixer_b:
            w_ret_up_b, w_pool_up_b, w_o_b = mixer_b
        merged = _mixer(proj, causal, chunk_decay, w_ret_up_b, w_pool_up_b, batch, seq)
        xt, xn = _residual_norm(xt, merged, w_o_b, norm2[layer][None, :], "out_proj", BF16)
        hidden, *next_b = _ffn_up(
            xn, w_g_b, w_u_b,
            lambda grid: [] if last else [
                _cast_stream(w, layer + 1, grid) for w in (w_in,) + mixer])
        if last:
            (out,) = _residual_norm(xt, hidden, w_down_b, norm_f[None, :], "ffn_down", F32,
                                    emit_residual=False)
        else:
            w_in_b, w_ret_up_b, w_pool_up_b, w_o_b = next_b
            xt, xn = _residual_norm(xt, hidden, w_down_b, norm1[layer + 1][None, :],
                                    "ffn_down", BF16)
    return out.reshape(batch, seq, D_MODEL)
```

```python
import functools
from typing import NamedTuple

import jax
import jax.numpy as jnp
from jax import lax
from jax.experimental import pallas as pl
from jax.experimental.pallas import tpu as pltpu

D_MODEL = 2048
RET_HEADS = 8
RET_QK_DIM = 128
RET_V_DIM = 256
RET_QK_WIDTH = RET_HEADS * RET_QK_DIM
RET_V_WIDTH = RET_HEADS * RET_V_DIM
RET_CHUNK = 128
ROPE_BASE = 10000.0
POOL_WINDOWS = (2, 4, 8, 16)
POOL_GROUPS = len(POOL_WINDOWS)
POOL_GROUP_DIM = D_MODEL // 8
POOL_WIDTH = POOL_GROUPS * POOL_GROUP_DIM
N_BRANCHES = 2
IN_WIDTH = 2 * RET_QK_WIDTH + 2 * RET_V_WIDTH + POOL_WIDTH + N_BRANCHES * D_MODEL
NORM_EPS = 1e-6

Q_OFF = 0
K_OFF = RET_QK_WIDTH
V_OFF = 2 * RET_QK_WIDTH
G_OFF = V_OFF + RET_V_WIDTH
U_OFF = G_OFF + RET_V_WIDTH
GATE_OFF = U_OFF + POOL_WIDTH

POOL_HALO = 16
assert POOL_HALO >= max(POOL_WINDOWS) and POOL_HALO % 8 == 0

VMEM_LIMIT_BYTES = 56 * 1024 * 1024
MXU_COLS = 256
assert POOL_GROUP_DIM == MXU_COLS

TOKEN_TILE = 1024
ROW_TILE = 512
FFN_TILE_COLS = 1024

BF16 = jnp.bfloat16
F32 = jnp.float32


def _params(n_axes):
    return pltpu.CompilerParams(
        dimension_semantics=("arbitrary",) * n_axes,
        vmem_limit_bytes=VMEM_LIMIT_BYTES,
    )


BF16_SUBLANES = 16


class _CastStream(NamedTuple):
    stacked: jax.Array
    in_spec: pl.BlockSpec
    out_specs: tuple
    out_shapes: tuple


def _cast_stream(stacked, layer, grid, n_split=1):
    _, k, n = stacked.shape
    n_steps = grid[0] * grid[1]
    rows = BF16_SUBLANES
    while k % rows or k // rows > n_steps:
        rows += BF16_SUBLANES
    last = k // rows - 1
    block = lambda i, j: jnp.minimum(i * grid[1] + j, last)
    return _CastStream(
        stacked,
        pl.BlockSpec((None, rows, n), lambda i, j: (layer, block(i, j), 0)),
        (pl.BlockSpec((rows, n // n_split), lambda i, j: (block(i, j), 0)),) * n_split,
        (jax.ShapeDtypeStruct((k, n // n_split), BF16),) * n_split,
    )


def _cast_out_specs(casts):
    return [s for c in casts for s in c.out_specs]


def _cast_out_shapes(casts):
    return [s for c in casts for s in c.out_shapes]


def _run_casts(src_refs, dst_refs, part=0, n_parts=1):
    d = 0
    for src in src_refs:
        col = 0
        while col < src.shape[1]:
            dst = dst_refs[d]
            width = dst.shape[1] // n_parts
            lo = part * width
            dst[:, lo:lo + width] = src[:, col + lo:col + lo + width].astype(dst.dtype)
            col += dst.shape[1]
            d += 1
    assert d == len(dst_refs)


def _sigmoid(x):
    return 0.5 * jnp.tanh(0.5 * x) + 0.5


def _silu(x):
    h = 0.5 * x
    return h + h * jnp.tanh(h)


def _rotary(t, cos, sin):
    outs = []
    for h in range(t.shape[1] // RET_QK_DIM):
        th = t[:, h * RET_QK_DIM:(h + 1) * RET_QK_DIM]
        outs.append(th * cos + pltpu.roll(th, RET_QK_DIM // 2, axis=1) * sin)
    return jnp.concatenate(outs, axis=1)


def _col_blocks(width):
    return [slice(c, c + MXU_COLS) for c in range(0, width, MXU_COLS)]


def _rmsnorm_rows(x_ref, gain_ref, out_ref):
    n_blocks = x_ref.shape[1] // MXU_COLS
    col = lambda c: slice(c * MXU_COLS, (c + 1) * MXU_COLS)
    sq = None
    for c in range(n_blocks):
        xc = x_ref[:, col(c)]
        sq = xc * xc if sq is None else sq + xc * xc
    scale = lax.rsqrt(jnp.sum(sq, axis=-1, keepdims=True) / x_ref.shape[1] + NORM_EPS)
    for c in range(n_blocks):
        out_ref[:, col(c)] = (x_ref[:, col(c)] * scale * gain_ref[:, col(c)]).astype(out_ref.dtype)


def _input_kernel(x_ref, gain_ref, pos_ref, freq_ref, xn_ref, cos_ref, sin_ref):
    _rmsnorm_rows(x_ref, gain_ref, xn_ref)
    half_rows = pos_ref.shape[0]
    ang = pos_ref[...] * freq_ref[...]
    lane = lax.broadcasted_iota(jnp.int32, ang.shape, 1)
    low = lane < RET_QK_DIM // 2
    cos = jnp.cos(ang)
    sin = jnp.sin(ang)
    cos_swapped = pltpu.roll(cos, RET_QK_DIM // 2, axis=1)
    sin_swapped = pltpu.roll(sin, RET_QK_DIM // 2, axis=1)
    cos_ref[:half_rows, :] = jnp.where(low, cos, cos_swapped)
    sin_ref[:half_rows, :] = jnp.where(low, -sin, sin_swapped)
    cos_ref[half_rows:, :] = jnp.where(low, cos_swapped, cos)
    sin_ref[half_rows:, :] = jnp.where(low, -sin_swapped, sin)


def _input_stage(x, gain, positions):
    t = x.shape[0]
    tm = TOKEN_TILE
    half = RET_QK_DIM // 2
    inv_freq = ROPE_BASE ** (-jnp.arange(half, dtype=F32) / half)
    freq = jnp.concatenate([inv_freq, inv_freq])[None, :]
    pos = positions.reshape(t // tm, 2, tm // 2).astype(F32).transpose(0, 2, 1)
    pos = jnp.broadcast_to(pos[..., None], (t // tm, tm // 2, 2, half)).reshape(t // 2, RET_QK_DIM)
    rows = lambda n, width: pl.BlockSpec((n, width), lambda i: (i, 0))
    whole = lambda width: pl.BlockSpec((1, width), lambda i: (0, 0))
    return pl.pallas_call(
        _input_kernel,
        out_shape=[jax.ShapeDtypeStruct(x.shape, BF16),
                   jax.ShapeDtypeStruct((t, RET_QK_DIM), F32),
                   jax.ShapeDtypeStruct((t, RET_QK_DIM), F32)],
        grid=(t // tm,),
        in_specs=[rows(tm, D_MODEL), whole(D_MODEL), rows(tm // 2, RET_QK_DIM), whole(RET_QK_DIM)],
        out_specs=[rows(tm, D_MODEL), rows(tm, RET_QK_DIM), rows(tm, RET_QK_DIM)],
        compiler_params=_params(1),
        name="input_stage",
    )(x, gain, pos, freq)


SECTION = 1024
SECTIONS_PER_STEP = 1
N_SECTIONS = IN_WIDTH // SECTION
assert all(off % SECTION == 0 for off in (K_OFF, V_OFF, G_OFF, U_OFF, GATE_OFF))
assert RET_QK_WIDTH == SECTION and POOL_WIDTH == SECTION


def _inproj_kernel(*refs, tiles_per_seq, n_cast_src, n_cast_dst):
    xn_ref, w_ref, cos_ref, sin_ref, decay_ref, pmix_ref, pscale_ref = refs[:7]
    cast_src = refs[7:7 + n_cast_src]
    o_ref = refs[7 + n_cast_src]
    cast_dst = refs[8 + n_cast_src:8 + n_cast_src + n_cast_dst]
    halo_ref = refs[8 + n_cast_src + n_cast_dst]
    i = pl.program_id(0)
    j = pl.program_id(1)
    tm = xn_ref.shape[0]

    def blockwise(cols, epilogue):
        n_blocks = (cols.stop - cols.start) // MXU_COLS
        xn = xn_ref[...]
        for c in range(n_blocks):
            blk = slice(cols.start + c * MXU_COLS, cols.start + (c + 1) * MXU_COLS)
            acc = jnp.dot(xn, w_ref[:, blk].astype(BF16), preferred_element_type=F32)
            o_ref[:, blk] = epilogue(acc, blk).astype(o_ref.dtype)
            _run_casts(cast_src, cast_dst, c, n_blocks)

    def rotary_section(cols, section):
        cos = cos_ref[...]
        sin = sin_ref[...]

        def epilogue(acc, blk):
            rot = _rotary(acc, cos, sin).reshape(tm // RET_CHUNK, RET_CHUNK, MXU_COLS)
            return (rot * decay_ref[section, :, blk]).reshape(tm, MXU_COLS)

        blockwise(cols, epilogue)

    def plain_section(cols):
        blockwise(cols, lambda acc, blk: acc)

    def silu_section(cols):
        blockwise(cols, lambda acc, blk: _silu(acc))

    def sigmoid_section(cols):
        blockwise(cols, lambda acc, blk: _sigmoid(acc))

    def pool_section(cols):
        seq_tile = i % tiles_per_seq

        @pl.when(seq_tile == 0)
        def _():
            halo_ref[...] = jnp.zeros_like(halo_ref)

        row = lax.broadcasted_iota(jnp.int32, (tm, 1), 0) + seq_tile * tm
        pos1 = (row + 1).astype(F32)
        xn = xn_ref[...]
        group = lambda gi: slice(gi * POOL_GROUP_DIM, (gi + 1) * POOL_GROUP_DIM)
        shifted = lambda gi: slice(cols.start + gi * POOL_GROUP_DIM,
                                   cols.start + (gi + 1) * POOL_GROUP_DIM)

        def pool(gi, u):
            s = jnp.concatenate([halo_ref[:, group(gi)], u], axis=0)
            halo_ref[:, group(gi)] = u[tm - POOL_HALO:, :]
            shift = 1
            while shift < POOL_WINDOWS[gi]:
                s = s + pltpu.roll(s, shift, axis=0)
                shift *= 2
            count = jnp.minimum(pos1, float(POOL_WINDOWS[gi]))
            return (s[POOL_HALO:, :] / count - u).astype(BF16)

        order = sorted(range(POOL_GROUPS), key=lambda gi: -POOL_WINDOWS[gi])
        pooled = {}
        for n, gi in enumerate(order):
            w = w_ref[:, shifted(gi)].astype(BF16)
            pooled[gi] = pool(gi, jnp.dot(xn, w, preferred_element_type=F32))
            _run_casts(cast_src, cast_dst, n, POOL_GROUPS)
        for gi in order:
            mixed = jnp.dot(pooled[gi], pmix_ref[gi], preferred_element_type=F32)
            o_ref[:, shifted(gi)] = (mixed * pscale_ref[:, group(gi)]).astype(o_ref.dtype)

    def section_fn(index):
        off = index * SECTION
        if off < V_OFF:
            return rotary_section
        if off < G_OFF:
            return plain_section
        if off < U_OFF:
            return silu_section
        if off == U_OFF:
            return pool_section
        return sigmoid_section

    n_steps = pl.cdiv(N_SECTIONS, SECTIONS_PER_STEP)
    step_fns = [tuple(section_fn(s * SECTIONS_PER_STEP + h) for h in range(SECTIONS_PER_STEP)
                      if s * SECTIONS_PER_STEP + h < N_SECTIONS) for s in range(n_steps)]
    lo = 0
    while lo < n_steps:
        hi = lo + 1
        while hi < n_steps and step_fns[hi] == step_fns[lo]:
            hi += 1

        @pl.when((j >= lo) & (j < hi))
        def _(fns=step_fns[lo]):
            for half, fn in enumerate(fns):
                cols = slice(half * SECTION, (half + 1) * SECTION)
                if fn is rotary_section:
                    fn(cols, j * SECTIONS_PER_STEP + half)
                else:
                    fn(cols)

        lo = hi


def _inproj(xn, w_in_any, cos, sin, qk_decay, pool_mix, pool_scale, layer, seq, casts_of):
    t = xn.shape[0]
    tm, tn = TOKEN_TILE, SECTION * SECTIONS_PER_STEP
    assert seq % tm == 0
    grid = (t // tm, pl.cdiv(IN_WIDTH, tn))
    casts = casts_of(grid)
    if w_in_any.ndim == 3:
        w_spec = pl.BlockSpec((None, D_MODEL, tn), lambda i, j: (layer, 0, j))
    else:
        w_spec = pl.BlockSpec((D_MODEL, tn), lambda i, j: (0, j))
    kern = functools.partial(_inproj_kernel, tiles_per_seq=seq // tm, n_cast_src=len(casts),
                             n_cast_dst=len(_cast_out_specs(casts)))
    return pl.pallas_call(
        kern,
        out_shape=[jax.ShapeDtypeStruct((t, IN_WIDTH), BF16)] + _cast_out_shapes(casts),
        grid=grid,
        in_specs=[
            pl.BlockSpec((tm, D_MODEL), lambda i, j: (i, 0)),
            w_spec,
            pl.BlockSpec((tm, RET_QK_DIM), lambda i, j: (i, 0)),
            pl.BlockSpec((tm, RET_QK_DIM), lambda i, j: (i, 0)),
            pl.BlockSpec(qk_decay.shape, lambda i, j: (0, 0, 0)),
            pl.BlockSpec((None, POOL_GROUPS, POOL_GROUP_DIM, POOL_GROUP_DIM),
                         lambda i, j: (layer, 0, 0, 0)),
            pl.BlockSpec((None, 1, POOL_WIDTH), lambda i, j: (layer, 0, 0)),
        ] + [c.in_spec for c in casts],
        out_specs=[pl.BlockSpec((tm, tn), lambda i, j: (i, j))] + _cast_out_specs(casts),
        scratch_shapes=[pltpu.VMEM((POOL_HALO, POOL_WIDTH), F32)],
        compiler_params=_params(2),
        name="in_proj",
    )(xn, w_in_any, cos, sin, qk_decay, pool_mix, pool_scale, *[c.stacked for c in casts])


def _retention_tables():
    c = RET_CHUNK
    log_gamma = jnp.log(1.0 - 2.0 ** (-5.0 - jnp.arange(RET_HEADS, dtype=F32)))
    idx = jnp.arange(c, dtype=F32)
    diff = idx[:, None] - idx[None, :]
    causal = jnp.where(diff[None] >= 0, jnp.exp(-c * log_gamma)[:, None, None], 0.0)
    xi = jnp.exp((idx + 1.0)[None, :] * log_gamma[:, None])
    zeta = jnp.exp((c - 1.0 - idx)[None, :] * log_gamma[:, None])
    chunk_decay = jnp.exp(c * log_gamma)
    per_head = lambda a: jnp.broadcast_to(a.T[:, :, None], (c, RET_HEADS, RET_QK_DIM)).reshape(
        c, RET_QK_WIDTH)
    qk_decay = jnp.stack([per_head(xi) * (RET_QK_DIM ** -0.5), per_head(zeta)])
    chunk_decay = jnp.broadcast_to(chunk_decay[:, None, None], (RET_HEADS, 1, RET_V_DIM))
    return qk_decay, causal, chunk_decay


GATE_BLOCK = 1024
assert GATE_OFF % GATE_BLOCK == 0 and D_MODEL % GATE_BLOCK == 0 and GATE_BLOCK % MXU_COLS == 0


def _mixer_kernel(*refs):
    qx_ref, kz_ref, v_ref, g_ref, yp_ref = refs[:5]
    n_gate = D_MODEL // GATE_BLOCK
    g0_refs = refs[5:5 + n_gate]
    g1_refs = refs[5 + n_gate:5 + 2 * n_gate]
    (causal_ref, cdecay_ref, wr_ref, wp_ref, o_ref,
     state_ref, yret_ref, pool_ref) = refs[5 + 2 * n_gate:]
    c = RET_CHUNK

    @pl.when(pl.program_id(1) == 0)
    def _():
        state_ref[...] = jnp.zeros_like(state_ref)

    def gate(gate_refs, cols):
        ref = gate_refs[cols.start // GATE_BLOCK]
        lo = cols.start % GATE_BLOCK
        return ref[:, lo:lo + MXU_COLS].astype(F32)

    out_blocks = _col_blocks(D_MODEL)
    n_chunks = qx_ref.shape[0] // c
    pool_blocks_per_chunk = pl.cdiv(len(out_blocks), n_chunks)
    yp = yp_ref[...]
    pending = list(out_blocks)

    def pool_branch(n_blocks):
        for _ in range(min(n_blocks, len(pending))):
            cols = pending.pop(0)
            p = jnp.dot(yp, wp_ref[:, cols], preferred_element_type=F32)
            pool_ref[:, cols] = gate(g1_refs, cols) * p

    heads = range(RET_HEADS)
    qk_cols = [slice(h * RET_QK_DIM, (h + 1) * RET_QK_DIM) for h in heads]
    v_cols = [slice(h * RET_V_DIM, (h + 1) * RET_V_DIM) for h in heads]
    for n in range(n_chunks):
        rows = slice(n * c, (n + 1) * c)
        qx = [qx_ref[rows, qk_cols[h]] for h in heads]
        kz = [kz_ref[rows, qk_cols[h]] for h in heads]
        v = [v_ref[rows, v_cols[h]] for h in heads]
        scores = [lax.dot_general(qx[h], kz[h], (((1,), (1,)), ((), ())),
                                  preferred_element_type=F32) for h in heads]
        kv = [lax.dot_general(kz[h], v[h], (((0,), (0,)), ((), ())),
                              preferred_element_type=F32) for h in heads]
        pool_branch(pool_blocks_per_chunk)
        state = [state_ref[h] for h in heads]
        lhs = [jnp.concatenate([(scores[h] * causal_ref[h]).astype(BF16), qx[h]], axis=1)
               for h in heads]
        rhs = [jnp.concatenate([v[h], state[h].astype(BF16)], axis=0) for h in heads]
        y = [jnp.dot(lhs[h], rhs[h], preferred_element_type=F32) for h in heads]
        for h in heads:
            state_ref[h] = state[h] * cdecay_ref[h] + kv[h]
            yn = y[h] * lax.rsqrt(jnp.mean(y[h] * y[h], axis=-1, keepdims=True) + NORM_EPS)
            yret_ref[rows, v_cols[h]] = yn.astype(yret_ref.dtype) * g_ref[rows, v_cols[h]]
    pool_branch(len(pending))

    yret = yret_ref[...]
    for cols in out_blocks:
        r = jnp.dot(yret, wr_ref[:, cols], preferred_element_type=F32)
        o_ref[:, cols] = (gate(g0_refs, cols) * r + pool_ref[:, cols]).astype(o_ref.dtype)


def _mixer(proj, causal, cdecay, w_ret_up_b, w_pool_up_b, batch, seq):
    t = proj.shape[0]
    ts = ROW_TILE
    spb = seq // ts
    tok_blk = lambda width, off: pl.BlockSpec((ts, width), lambda b, s: (b * spb + s, off // width))
    full_blk = lambda a: pl.BlockSpec(a.shape, lambda b, s: (0,) * a.ndim)
    weight_blk = lambda a: pl.BlockSpec(a.shape, lambda b, s: (0, 0), pipeline_mode=pl.Buffered(1))
    gate_blks = [tok_blk(GATE_BLOCK, GATE_OFF + branch * D_MODEL + part * GATE_BLOCK)
                 for branch in range(N_BRANCHES) for part in range(D_MODEL // GATE_BLOCK)]
    return pl.pallas_call(
        _mixer_kernel,
        out_shape=jax.ShapeDtypeStruct((t, D_MODEL), BF16),
        grid=(batch, spb),
        in_specs=[
            tok_blk(RET_QK_WIDTH, Q_OFF), tok_blk(RET_QK_WIDTH, K_OFF),
            tok_blk(RET_V_WIDTH, V_OFF), tok_blk(RET_V_WIDTH, G_OFF),
            tok_blk(POOL_WIDTH, U_OFF),
            *gate_blks,
            full_blk(causal), full_blk(cdecay),
            weight_blk(w_ret_up_b), weight_blk(w_pool_up_b),
        ],
        out_specs=tok_blk(D_MODEL, 0),
        scratch_shapes=[
            pltpu.VMEM((RET_HEADS, RET_QK_DIM, RET_V_DIM), F32),
            pltpu.VMEM((ts, RET_V_WIDTH), BF16),
            pltpu.VMEM((ts, D_MODEL), F32),
        ],
        compiler_params=_params(2),
        name="mixer",
    )(proj, proj, proj, proj, proj, *([proj] * len(gate_blks)),
      causal, cdecay, w_ret_up_b, w_pool_up_b)


def _residual_norm_kernel(*refs, emit_residual):
    x_ref, a_ref, w_ref, gain_ref = refs[:4]
    if emit_residual:
        res_ref, xn_ref = refs[4:]
    else:
        (xn_ref,) = refs[4:]
        res_ref = xn_ref
    a = a_ref[...]
    for c in range(res_ref.shape[1] // MXU_COLS):
        cols = slice(c * MXU_COLS, (c + 1) * MXU_COLS)
        res_ref[:, cols] = x_ref[:, cols] + jnp.dot(a, w_ref[:, cols], preferred_element_type=F32)
    _rmsnorm_rows(res_ref, gain_ref, xn_ref)


def _residual_norm(x, a, w_b, gain, name, xn_dtype, emit_residual=True):
    t, k = a.shape
    tm = ROW_TILE
    row_blk = lambda: pl.BlockSpec((tm, D_MODEL), lambda i: (i, 0))
    out_shape = [jax.ShapeDtypeStruct(x.shape, xn_dtype)]
    out_specs = [row_blk()]
    if emit_residual:
        out_shape.insert(0, jax.ShapeDtypeStruct(x.shape, x.dtype))
        out_specs.insert(0, row_blk())
    return pl.pallas_call(
        functools.partial(_residual_norm_kernel, emit_residual=emit_residual),
        out_shape=out_shape,
        grid=(t // tm,),
        in_specs=[
            row_blk(),
            pl.BlockSpec((tm, k), lambda i: (i, 0)),
            pl.BlockSpec((k, D_MODEL), lambda i: (0, 0), pipeline_mode=pl.Buffered(1)),
            pl.BlockSpec((1, D_MODEL), lambda i: (0, 0)),
        ],
        out_specs=out_specs,
        compiler_params=_params(1),
        name=name,
    )(x, a, w_b, gain)


def _ffn_up_kernel(*refs, d_ff, n_cast_src, n_cast_dst):
    xn_ref, wg_ref, wu_ref = refs[:3]
    cast_src = refs[3:3 + n_cast_src]
    o_ref = refs[3 + n_cast_src]
    cast_dst = refs[4 + n_cast_src:4 + n_cast_src + n_cast_dst]
    j = pl.program_id(1)
    tn = o_ref.shape[1]

    def body(n_blocks):
        xn = xn_ref[...]
        for c in range(n_blocks):
            cols = slice(c * MXU_COLS, (c + 1) * MXU_COLS)
            gate = jnp.dot(xn, wg_ref[:, cols], preferred_element_type=F32)
            up = jnp.dot(xn, wu_ref[:, cols], preferred_element_type=F32)
            o_ref[:, cols] = (_silu(gate) * up).astype(o_ref.dtype)
            _run_casts(cast_src, cast_dst, c, n_blocks)

    full_steps, tail = divmod(d_ff, tn)

    @pl.when(j < full_steps)
    def _():
        body(tn // MXU_COLS)

    if tail:
        @pl.when(j == full_steps)
        def _():
            body(tail // MXU_COLS)


def _ffn_up(xn, w_g_b, w_u_b, casts_of):
    t = xn.shape[0]
    d_ff = w_g_b.shape[1]
    tm, tn = TOKEN_TILE, FFN_TILE_COLS
    assert d_ff % MXU_COLS == 0
    grid = (t // tm, pl.cdiv(d_ff, tn))
    casts = casts_of(grid)
    kern = functools.partial(_ffn_up_kernel, d_ff=d_ff, n_cast_src=len(casts),
                             n_cast_dst=len(_cast_out_specs(casts)))
    return pl.pallas_call(
        kern,
        out_shape=[jax.ShapeDtypeStruct((t, d_ff), BF16)] + _cast_out_shapes(casts),
        grid=grid,
        in_specs=[
            pl.BlockSpec((tm, D_MODEL), lambda i, j: (i, 0)),
            pl.BlockSpec((D_MODEL, tn), lambda i, j: (0, j)),
            pl.BlockSpec((D_MODEL, tn), lambda i, j: (0, j)),
        ] + [c.in_spec for c in casts],
        out_specs=[pl.BlockSpec((tm, tn), lambda i, j: (i, j))] + _cast_out_specs(casts),
        compiler_params=_params(2),
        name="ffn_up",
    )(xn, w_g_b, w_u_b, *[c.stacked for c in casts])


def kernel(x, positions, norm1, w_in, pool_mix, pool_scale, w_ret_up, w_pool_up, w_o,
           norm2, w_gu, w_down, norm_f):
    batch, seq, _ = x.shape
    depth = w_in.shape[0]
    t = batch * seq
    xt = x.reshape(t, D_MODEL)

    pool_mix_b = pool_mix.astype(BF16)
    pool_scale_r = pool_scale.reshape(depth, 1, POOL_WIDTH)

    qk_decay, causal, chunk_decay = _retention_tables()

    w_in_b = w_in
    xn, cos, sin = _input_stage(xt, norm1[0][None, :], positions)
    mixer = (w_ret_up, w_pool_up, w_o)
    for layer in range(depth):
        last = layer + 1 == depth
        proj, w_g_b, w_u_b, w_down_b, *mixer_b = _inproj(
            xn, w_in_b, cos, sin, qk_decay, pool_mix_b, pool_scale_r, layer, seq,
            lambda grid: [_cast_stream(w_gu, layer, grid, n_split=2),
                          _cast_stream(w_down, layer, grid)]
            + [_cast_stream(w, layer, grid) for w in (mixer if layer == 0 else ())])
        if mixer_b:
            w_ret_up_b, w_pool_up_b, w_o_b = mixer_b
        merged = _mixer(proj, causal, chunk_decay, w_ret_up_b, w_pool_up_b, batch, seq)
        xt, xn = _residual_norm(xt, merged, w_o_b, norm2[layer][None, :], "out_proj", BF16)
        hidden, *next_b = _ffn_up(
            xn, w_g_b, w_u_b,
            lambda grid: [] if last else [
                _cast_stream(w, layer + 1, grid) for w in (w_in,) + mixer])
        if last:
            (out,) = _residual_norm(xt, hidden, w_down_b, norm_f[None, :], "ffn_down", F32,
                                    emit_residual=False)
        else:
            w_in_b, w_ret_up_b, w_pool_up_b, w_o_b = next_b
            xt, xn = _residual_norm(xt, hidden, w_down_b, norm1[layer + 1][None, :],
                                    "ffn_down", BF16)
    return out.reshape(batch, seq, D_MODEL)
```
